```python
import jax, jax.numpy as jnp
from jax import lax
import numpy as np

D_MODEL = 1024
BATCH = 8
SEQ = 2048
DEPTH = 1
DEC_BATCH = 128
DEC_SEQ = 4
PAST_LEN = 16384
PAGE_SIZE = 128

RG_WIDTH = D_MODEL
RG_BLOCKS = 8
RG_BLOCK = RG_WIDTH // RG_BLOCKS
RG_C = 8.0
CONV_W = 4
SSD_EXPAND = 2
SSD_INNER = SSD_EXPAND * D_MODEL
SSD_HEAD_DIM = 64
SSD_HEADS = SSD_INNER // SSD_HEAD_DIM
SSD_GROUPS = 8
SSD_HPG = SSD_HEADS // SSD_GROUPS
SSD_STATE = 128
SSD_CHUNK = 128
SSD_CONV_DIM = SSD_INNER + 2 * SSD_GROUPS * SSD_STATE
D_FF = 2816
EPS = 1e-6
IN_SIZES = (RG_WIDTH, RG_WIDTH, SSD_INNER, SSD_CONV_DIM, SSD_HEADS, D_MODEL, D_MODEL)
D_IN = sum(IN_SIZES)

kernel_name = "hawk_ssd_parallel_macaron_decoder_step"


def rmsnorm(x, w):
    xf = x.astype(jnp.float32)
    y = xf * lax.rsqrt(jnp.mean(xf * xf, axis=-1, keepdims=True) + EPS)
    return (y * w.astype(jnp.float32)).astype(x.dtype)


def swiglu(x, wg, wu, wd):
    return (jax.nn.silu(x @ wg) * (x @ wu)) @ wd


def causal_conv(x, buf, w, b):
    L = x.shape[1]
    xc = jnp.concatenate([buf.astype(x.dtype), x], axis=1)
    y = xc[:, 0:L] * w[0]
    for k in range(1, CONV_W):
        y = y + xc[:, k:k + L] * w[k]
    return y + b, xc[:, -(CONV_W - 1):]


def rg_lru(x, pos, h0, wa, ba, wx, bx, lam):
    b, l, _ = x.shape
    xf = x.astype(jnp.float32)
    xb = xf.reshape(b, l, RG_BLOCKS, RG_BLOCK)
    gate_a = jax.nn.sigmoid(jnp.einsum('blhi,hij->blhj', xb, wa).reshape(b, l, RG_WIDTH) + ba)
    gate_x = jax.nn.sigmoid(jnp.einsum('blhi,hij->blhj', xb, wx).reshape(b, l, RG_WIDTH) + bx)
    log_a = -RG_C * gate_a * jax.nn.softplus(-lam.astype(jnp.float32))
    reset = (pos == 0)[None, :, None]
    a = jnp.where(reset, 0.0, jnp.exp(log_a))
    mult = jnp.where(reset, 1.0, jnp.sqrt(-jnp.expm1(2.0 * log_a)))
    u = xf * gate_x * mult

    def combine(lft, rgt):
        return (lft[0] * rgt[0], rgt[0] * lft[1] + rgt[1])

    a_cum, u_cum = lax.associative_scan(combine, (a, u), axis=1)
    h = a_cum * h0.astype(jnp.float32)[:, None] + u_cum
    return h, h[:, -1]


def ssd_scan(x, dt, A, B, C, h0):
    b, l = x.shape[:2]
    q = min(SSD_CHUNK, l)
    nc = -(-l // q)
    pad = nc * q - l
    if pad:
        pw = lambda t: jnp.pad(t, [(0, 0), (0, pad)] + [(0, 0)] * (t.ndim - 2))
        x, dt, B, C = pw(x), pw(dt), pw(B), pw(C)
    G, R, P, N = SSD_GROUPS, SSD_HPG, SSD_HEAD_DIM, SSD_STATE
    x = x.reshape(b, nc, q, G, R, P)
    dt = dt.reshape(b, nc, q, G, R)
    B = B.reshape(b, nc, q, G, N)
    C = C.reshape(b, nc, q, G, N)
    acs = jnp.cumsum(dt * A.reshape(G, R), axis=2)
    seg = acs[:, :, :, None] - acs[:, :, None, :]
    mask = jnp.tril(jnp.ones((q, q), bool))[:, :, None, None]
    lmat = jnp.exp(jnp.where(mask, seg, -jnp.inf))
    cb = jnp.einsum('bcign,bcjgn->bcijg', C, B)
    w = cb[..., None] * lmat * dt[:, :, None]
    y_diag = jnp.einsum('bcijgr,bcjgrp->bcigrp', w, x)
    decay = jnp.exp(acs[:, :, -1:] - acs)
    chunk_states = jnp.einsum('bcjgn,bcjgr,bcjgrp->bcgrpn', B, decay * dt, x)
    chunk_decay = jnp.exp(acs[:, :, -1])

    def step(s, inp):
        cs, cd = inp
        return cd[..., None, None] * s + cs, s

    final, s_in = lax.scan(step, h0.astype(jnp.float32).reshape(b, G, R, P, N),
                           (jnp.moveaxis(chunk_states, 1, 0), jnp.moveaxis(chunk_decay, 1, 0)))
    s_in = jnp.moveaxis(s_in, 0, 1)
    y_off = jnp.einsum('bcign,bcgrpn->bcigrp', C, s_in) * jnp.exp(acs)[..., None]
    y = (y_diag + y_off).reshape(b, nc * q, SSD_HEADS, P)[:, :l]
    return y, final.reshape(b, SSD_HEADS, P, N)


def mixer(u, pos, rg_h0, rg_buf, ssd_h0, ssd_buf, p):
    b, l, _ = u.shape
    f32 = jnp.float32
    proj = u @ p['w_in']
    cuts = [int(c) for c in np.cumsum(IN_SIZES)[:-1]]
    rg_x, rg_g, z, xbc, dt_raw, g_rg, g_ssd = jnp.split(proj, cuts, axis=-1)
    rg_xc, rg_buf_new = causal_conv(rg_x, rg_buf, p['rg_conv_w'], p['rg_conv_b'])
    h, rg_h_new = rg_lru(rg_xc, pos, rg_h0, p['rg_wa'], p['rg_ba'], p['rg_wx'], p['rg_bx'], p['rg_lambda'])
    y_rg = h * jax.nn.gelu(rg_g.astype(f32))
    xbc_c, ssd_buf_new = causal_conv(xbc, ssd_buf, p['ssd_conv_w'], p['ssd_conv_b'])
    xbc_c = jax.nn.silu(xbc_c.astype(f32))
    xs, Bs, Cs = jnp.split(xbc_c, [SSD_INNER, SSD_INNER + SSD_GROUPS * SSD_STATE], axis=-1)
    dt = jax.nn.softplus(dt_raw.astype(f32) + p['ssd_dt_bias'])
    A = -jnp.exp(p['ssd_a_log'].astype(f32))
    xh = xs.reshape(b, l, SSD_HEADS, SSD_HEAD_DIM)
    y, ssd_h_new = ssd_scan(xh, dt, A, Bs.reshape(b, l, SSD_GROUPS, SSD_STATE),
                            Cs.reshape(b, l, SSD_GROUPS, SSD_STATE), ssd_h0)
    y = (y + p['ssd_d'][:, None] * xh).reshape(b, l, SSD_INNER) * jax.nn.silu(z.astype(f32))
    yg = y.reshape(b, l, SSD_GROUPS, SSD_INNER // SSD_GROUPS)
    yg = yg * lax.rsqrt(jnp.mean(yg * yg, axis=-1, keepdims=True) + EPS)
    y_ssd = yg.reshape(b, l, SSD_INNER) * p['ssd_norm_w']
    m = (jax.nn.sigmoid(g_rg.astype(f32)) * (y_rg @ p['w_proj_rg'])
         + jax.nn.sigmoid(g_ssd.astype(f32)) * (y_ssd @ p['w_proj_ssd']))
    out = (m @ p['w_out']).astype(u.dtype)
    new = (rg_h_new.astype(rg_h0.dtype), rg_buf_new.astype(rg_buf.dtype),
           ssd_h_new.astype(ssd_h0.dtype), ssd_buf_new.astype(ssd_buf.dtype))
    return out, new


def layer(x, pos, rg_h0, rg_buf, ssd_h0, ssd_buf, p):
    x = x + 0.5 * rmsnorm(swiglu(rmsnorm(x, p['n_ffn1_pre']), p['ffn1_wg'], p['ffn1_wu'], p['ffn1_wd']),
                          p['n_ffn1_post'])
    mix, new = mixer(rmsnorm(x, p['n_mix_pre']), pos, rg_h0, rg_buf, ssd_h0, ssd_buf, p)
    x = x + rmsnorm(mix, p['n_mix_post'])
    x = x + 0.5 * rmsnorm(swiglu(rmsnorm(x, p['n_ffn2_pre']), p['ffn2_wg'], p['ffn2_wu'], p['ffn2_wd']),
                          p['n_ffn2_post'])
    return x, new


def setup_inputs(seed: int = 0) -> dict:
    key = jax.random.key(seed)
    ks = iter(jax.random.split(key, 48))
    f32 = jnp.float32
    nrm = lambda shape, s: jax.random.normal(next(ks), shape, f32) * s
    gain = lambda n: 1.0 + nrm((DEPTH, n), 0.05)
    d = {}
    d['x_prompt'] = nrm((BATCH, SEQ, D_MODEL), 1.0)
    d['x_sample'] = nrm((DEC_BATCH, DEC_SEQ, D_MODEL), 1.0)
    d['state_rg_h'] = nrm((DEPTH, DEC_BATCH, RG_WIDTH), 0.5)
    d['state_rg_conv'] = nrm((DEPTH, DEC_BATCH, CONV_W - 1, RG_WIDTH), 1.0)
    d['state_ssd'] = nrm((DEPTH, DEC_BATCH, SSD_HEADS, SSD_HEAD_DIM, SSD_STATE), 0.3)
    d['state_ssd_conv'] = nrm((DEPTH, DEC_BATCH, CONV_W - 1, SSD_CONV_DIM), 1.0)
    d['n_ffn1_pre'] = gain(D_MODEL)
    d['n_ffn1_post'] = gain(D_MODEL)
    d['ffn1_wg'] = nrm((DEPTH, D_MODEL, D_FF), D_MODEL ** -0.5)
    d['ffn1_wu'] = nrm((DEPTH, D_MODEL, D_FF), D_MODEL ** -0.5)
    d['ffn1_wd'] = nrm((DEPTH, D_FF, D_MODEL), D_FF ** -0.5)
    d['n_mix_pre'] = gain(D_MODEL)
    d['n_mix_post'] = gain(D_MODEL)
    d['w_in'] = nrm((DEPTH, D_MODEL, D_IN), D_MODEL ** -0.5)
    d['rg_conv_w'] = nrm((DEPTH, CONV_W, RG_WIDTH), CONV_W ** -0.5)
    d['rg_conv_b'] = nrm((DEPTH, RG_WIDTH), 0.02)
    d['rg_wa'] = nrm((DEPTH, RG_BLOCKS, RG_BLOCK, RG_BLOCK), RG_BLOCK ** -0.5)
    d['rg_ba'] = nrm((DEPTH, RG_WIDTH), 0.1)
    d['rg_wx'] = nrm((DEPTH, RG_BLOCKS, RG_BLOCK, RG_BLOCK), RG_BLOCK ** -0.5)
    d['rg_bx'] = nrm((DEPTH, RG_WIDTH), 0.1)
    a_c = jax.random.uniform(next(ks), (DEPTH, RG_WIDTH), f32, 0.9, 0.999)
    s = a_c ** (1.0 / RG_C)
    d['rg_lambda'] = jnp.log(s) - jnp.log1p(-s)
    d['ssd_conv_w'] = nrm((DEPTH, CONV_W, SSD_CONV_DIM), CONV_W ** -0.5)
    d['ssd_conv_b'] = nrm((DEPTH, SSD_CONV_DIM), 0.02)
    dt0 = jnp.exp(jax.random.uniform(next(ks), (DEPTH, SSD_HEADS), f32, np.log(1e-3), np.log(1e-1)))
    d['ssd_dt_bias'] = dt0 + jnp.log(-jnp.expm1(-dt0))
    d['ssd_a_log'] = jnp.log(jax.random.uniform(next(ks), (DEPTH, SSD_HEADS), f32, 1.0, 16.0))
    d['ssd_d'] = 1.0 + nrm((DEPTH, SSD_HEADS), 0.1)
    d['ssd_norm_w'] = gain(SSD_INNER)
    d['w_proj_rg'] = nrm((DEPTH, RG_WIDTH, D_MODEL), RG_WIDTH ** -0.5)
    d['w_proj_ssd'] = nrm((DEPTH, SSD_INNER, D_MODEL), SSD_INNER ** -0.5)
    d['w_out'] = nrm((DEPTH, D_MODEL, D_MODEL), D_MODEL ** -0.5)
    d['n_ffn2_pre'] = gain(D_MODEL)
    d['n_ffn2_post'] = gain(D_MODEL)
    d['ffn2_wg'] = nrm((DEPTH, D_MODEL, D_FF), D_MODEL ** -0.5)
    d['ffn2_wu'] = nrm((DEPTH, D_MODEL, D_FF), D_MODEL ** -0.5)
    d['ffn2_wd'] = nrm((DEPTH, D_FF, D_MODEL), D_FF ** -0.5)
    return d


def reference(x_prompt, x_sample, state_rg_h, state_rg_conv, state_ssd, state_ssd_conv,
              n_ffn1_pre, n_ffn1_post, ffn1_wg, ffn1_wu, ffn1_wd, n_mix_pre, n_mix_post, w_in,
              rg_conv_w, rg_conv_b, rg_wa, rg_ba, rg_wx, rg_bx, rg_lambda,
              ssd_conv_w, ssd_conv_b, ssd_dt_bias, ssd_a_log, ssd_d, ssd_norm_w,
              w_proj_rg, w_proj_ssd, w_out, n_ffn2_pre, n_ffn2_post, ffn2_wg, ffn2_wu, ffn2_wd):
    params = dict(n_ffn1_pre=n_ffn1_pre, n_ffn1_post=n_ffn1_post, ffn1_wg=ffn1_wg, ffn1_wu=ffn1_wu,
                  ffn1_wd=ffn1_wd, n_mix_pre=n_mix_pre, n_mix_post=n_mix_post, w_in=w_in,
                  rg_conv_w=rg_conv_w, rg_conv_b=rg_conv_b, rg_wa=rg_wa, rg_ba=rg_ba, rg_wx=rg_wx,
                  rg_bx=rg_bx, rg_lambda=rg_lambda, ssd_conv_w=ssd_conv_w, ssd_conv_b=ssd_conv_b,
                  ssd_dt_bias=ssd_dt_bias, ssd_a_log=ssd_a_log, ssd_d=ssd_d, ssd_norm_w=ssd_norm_w,
                  w_proj_rg=w_proj_rg, w_proj_ssd=w_proj_ssd, w_out=w_out, n_ffn2_pre=n_ffn2_pre,
                  n_ffn2_post=n_ffn2_post, ffn2_wg=ffn2_wg, ffn2_wu=ffn2_wu, ffn2_wd=ffn2_wd)
    bp, lp = x_prompt.shape[:2]
    ls = x_sample.shape[1]
    pos_p = jnp.arange(lp, dtype=jnp.int32)
    pos_s = PAST_LEN + jnp.arange(ls, dtype=jnp.int32)
    yp, ys = x_prompt, x_sample
    p_new = ([], [], [], [])
    s_new = ([], [], [], [])
    for li in range(DEPTH):
        p = {k: v[li] for k, v in params.items()}
        z_rg_h = jnp.zeros((bp,) + state_rg_h.shape[2:], state_rg_h.dtype)
        z_rg_c = jnp.zeros((bp,) + state_rg_conv.shape[2:], state_rg_conv.dtype)
        z_ssd = jnp.zeros((bp,) + state_ssd.shape[2:], state_ssd.dtype)
        z_ssd_c = jnp.zeros((bp,) + state_ssd_conv.shape[2:], state_ssd_conv.dtype)
        yp, newp = layer(yp, pos_p, z_rg_h, z_rg_c, z_ssd, z_ssd_c, p)
        ys, news = layer(ys, pos_s, state_rg_h[li], state_rg_conv[li], state_ssd[li], state_ssd_conv[li], p)
        for lst, v in zip(p_new, newp):
            lst.append(v)
        for lst, v in zip(s_new, news):
            lst.append(v)
    prompt_rg_h, prompt_rg_conv, prompt_ssd, prompt_ssd_conv = [jnp.stack(v, 0) for v in p_new]
    sample_rg_h, sample_rg_conv, sample_ssd, sample_ssd_conv = [jnp.stack(v, 0) for v in s_new]
    return (yp, ys, prompt_rg_h, prompt_rg_conv, prompt_ssd, prompt_ssd_conv,
            sample_rg_h, sample_rg_conv, sample_ssd, sample_ssd_conv)
```

```python
import functools

import jax
import jax.numpy as jnp
from jax import lax
from jax.experimental import pallas as pl
from jax.experimental.pallas import tpu as pltpu

F32 = jnp.float32
BF16 = jnp.bfloat16

EPS = 1e-6
RG_C = 8.0
CONV_W = 4
RG_BLOCKS = 8
SSD_HEAD_DIM = 64
SSD_GROUPS = 8
SSD_STATE = 128
SSD_CHUNK = 128
LANES = 128
SUBLANES = 8
VMEM_LIMIT_BYTES = 58 * 1024 * 1024


def _dot(a, b):
    return jnp.dot(a, b, preferred_element_type=F32)


def _rms(x, w):
    return x * lax.rsqrt(jnp.mean(x * x, axis=-1, keepdims=True) + EPS) * w


def _sigmoid(x):
    return 1.0 / (1.0 + jnp.exp(-x))


def _silu(x):
    return x * _sigmoid(x)


def _softplus(x):
    return jnp.maximum(x, 0.0) + jnp.log1p(jnp.exp(-jnp.abs(x)))


def _neg_expm1_2x(x, ex):
    e2 = ex * ex
    return jnp.where(jnp.abs(x) > 0.25, 1.0 - e2, -jnp.tanh(x) * (e2 + 1.0))


def _gelu_tanh(x):
    return 0.5 * x * (1.0 + jnp.tanh(0.7978845608028654 * (x + 0.044715 * (x * x * x))))


def _split3(v):
    hi = v.astype(BF16)
    r1 = v - hi.astype(F32)
    mid = r1.astype(BF16)
    lo = (r1 - mid.astype(F32)).astype(BF16)
    return hi, mid, lo


def _expand_heads(v, e_ref):
    e = e_ref[...]
    hi, mid, lo = _split3(v)
    return _dot(hi, e) + _dot(mid, e) + _dot(lo, e)


def _const_spec(shape):
    nd = len(shape)
    return pl.BlockSpec(shape, lambda *_: (0,) * nd, pipeline_mode=pl.Buffered(1))


def _ffn_body(x_ref, pre_ref, post_ref, wg_ref, wu_ref, wd_ref, o_ref):
    x = x_ref[...]
    u = _rms(x, pre_ref[...]).astype(BF16)
    g = _dot(u, wg_ref[...])
    up = _dot(u, wu_ref[...])
    h = (_silu(g) * up).astype(BF16)
    y = _dot(h, wd_ref[...])
    o_ref[...] = x + 0.5 * _rms(y, post_ref[...])


def _ffn_call(x2d, pre, post, wg, wu, wd, tm):
    t, d = x2d.shape
    assert t % tm == 0
    return pl.pallas_call(
        _ffn_body,
        grid=(t // tm,),
        in_specs=[pl.BlockSpec((tm, d), lambda i: (i, 0)),
                  _const_spec(pre.shape), _const_spec(post.shape),
                  _const_spec(wg.shape), _const_spec(wu.shape), _const_spec(wd.shape)],
        out_specs=pl.BlockSpec((tm, d), lambda i: (i, 0)),
        out_shape=jax.ShapeDtypeStruct((t, d), F32),
        compiler_params=pltpu.CompilerParams(dimension_semantics=("arbitrary",),
                                             vmem_limit_bytes=VMEM_LIMIT_BYTES),
        name="ffn_half_step",
    )(x2d, pre, post, wg, wu, wd)


def _rg_gates(xc, wcat_ref, ba, bx, lam):
    blk = xc.shape[1] // RG_BLOCKS
    xb = xc.astype(BF16)
    ga, gx = [], []
    for h in range(RG_BLOCKS):
        r = _dot(xb[:, h * blk:(h + 1) * blk], wcat_ref[h])
        ga.append(r[:, :blk])
        gx.append(r[:, blk:])
    gate_a = _sigmoid(jnp.concatenate(ga, axis=1) + ba)
    gate_x = _sigmoid(jnp.concatenate(gx, axis=1) + bx)
    log_a = (-RG_C) * gate_a * _softplus(-lam)
    return log_a, gate_x


def _group_rmsnorm(y, nw, groups):
    gw = y.shape[1] // groups
    outs = []
    for g in range(groups):
        yg = y[:, g * gw:(g + 1) * gw]
        outs.append(yg * lax.rsqrt(jnp.mean(yg * yg, axis=-1, keepdims=True) + EPS))
    return jnp.concatenate(outs, axis=1) * nw


def _merge_out(x_res, y_rg, y_ssd, gates, wprg_ref, wpssd_ref, wout_ref, npost):
    width = y_rg.shape[1]
    m = (_sigmoid(gates[:, :width]) * _dot(y_rg.astype(BF16), wprg_ref[...])
         + _sigmoid(gates[:, width:]) * _dot(y_ssd.astype(BF16), wpssd_ref[...]))
    out = _dot(m.astype(BF16), wout_ref[...])
    return x_res + _rms(out, npost)


def _conv_tile(x, cbuf_ref, w_ref, b_ref):
    tm = x.shape[0]
    cbuf_ref[pl.ds(SUBLANES, tm), :] = x
    y = x * w_ref[CONV_W - 1:CONV_W, :] + b_ref[...]
    for s in range(1, CONV_W):
        y = y + cbuf_ref[pl.ds(SUBLANES - s, tm), :] * w_ref[CONV_W - 1 - s:CONV_W - s, :]
    cbuf_ref[pl.ds(0, SUBLANES), :] = cbuf_ref[pl.ds(tm, SUBLANES), :]
    return y


def _prompt_mixer_body(x_ref, npre_ref, npost_ref, wmain_ref, wdt_ref, wgate_ref,
                       rcw_ref, rcb_ref, wcat_ref, ba_ref, bx_ref, lam_ref,
                       scw_ref, scb_ref, dtb_ref, alog_ref, dexp_ref, nw_ref, e_ref,
                       wprg_ref, wpssd_ref, wout_ref,
                       o_ref, orgh_ref, orgc_ref, ossd_ref, osc_ref,
                       cb_rg, cb_xbc, a_s, u_s, h_s, hc_s, y_s, st_s,
                       *, tm, width, inner):
    b = pl.program_id(0)
    i = pl.program_id(1)
    nt = pl.num_programs(1)
    nch = tm // SSD_CHUNK
    gn = SSD_GROUPS * SSD_STATE
    heads = inner // SSD_HEAD_DIM
    gw = inner // SSD_GROUPS
    hpg = heads // SSD_GROUPS

    @pl.when(i == 0)
    def _():
        cb_rg[pl.ds(0, SUBLANES), :] = jnp.zeros((SUBLANES, width), F32)
        cb_xbc[pl.ds(0, SUBLANES), :] = jnp.zeros((SUBLANES, inner + 2 * gn), F32)
        hc_s[...] = jnp.zeros_like(hc_s)
        st_s[...] = jnp.zeros_like(st_s)

    x = x_ref[0]
    u = _rms(x, npre_ref[...]).astype(BF16)

    rg_x = _dot(u, wmain_ref[:, 0:width])
    xc = _conv_tile(rg_x, cb_rg, rcw_ref, rcb_ref)
    log_a, gate_x = _rg_gates(xc, wcat_ref, ba_ref[...], bx_ref[...], lam_ref[...])
    row = lax.broadcasted_iota(jnp.int32, (tm, 1), 0)
    is_reset = jnp.logical_and(i == 0, row == 0)
    a = jnp.exp(log_a)
    a_s[...] = jnp.where(is_reset, 0.0, a)
    u_s[...] = xc * gate_x * jnp.where(is_reset, 1.0, jnp.sqrt(_neg_expm1_2x(log_a, a)))

    def scan_step(t, h):
        h = a_s[pl.ds(t, 1), :] * h + u_s[pl.ds(t, 1), :]
        h_s[pl.ds(t, 1), :] = h
        return h

    hc_s[...] = lax.fori_loop(0, tm, scan_step, hc_s[...], unroll=8)
    rg_g = _dot(u, wmain_ref[:, width:2 * width])
    y_rg = h_s[...] * _gelu_tanh(rg_g)

    xbc = _dot(u, wmain_ref[:, 2 * width + inner:2 * width + 2 * inner + 2 * gn])
    xbc = _silu(_conv_tile(xbc, cb_xbc, scw_ref, scb_ref))
    lane = lax.broadcasted_iota(jnp.int32, (1, LANES), 1)
    dt_all = jnp.where(lane < heads, _softplus(_dot(u, wdt_ref[...]) + dtb_ref[...]), 0.0)
    da_all = dt_all * (-jnp.exp(alog_ref[...]))
    ri = lax.broadcasted_iota(jnp.int32, (SSD_CHUNK, SSD_CHUNK), 0)
    ci = lax.broadcasted_iota(jnp.int32, (SSD_CHUNK, SSD_CHUNK), 1)
    causal = ri >= ci
    tri = causal.astype(F32)
    glane = lax.broadcasted_iota(jnp.int32, (1, gw), 1) // SSD_HEAD_DIM

    for c in range(nch):
        r0 = c * SSD_CHUNK
        xs = xbc[r0:r0 + SSD_CHUNK, 0:inner]
        bs = xbc[r0:r0 + SSD_CHUNK, inner:inner + gn].astype(BF16)
        cs = xbc[r0:r0 + SSD_CHUNK, inner + gn:inner + 2 * gn].astype(BF16)
        dt = dt_all[r0:r0 + SSD_CHUNK, :]
        acs = jnp.dot(tri, da_all[r0:r0 + SSD_CHUNK, :], preferred_element_type=F32,
                      precision=lax.Precision.HIGHEST)
        acs_t = acs.T
        dt_t = dt.T
        last = acs[SSD_CHUNK - 1:SSD_CHUNK, :]
        wdec_x = _expand_heads(jnp.exp(last - acs) * dt, e_ref)
        eacs_x = _expand_heads(jnp.exp(acs), e_ref)
        cd_x = _expand_heads(jnp.broadcast_to(jnp.exp(last), (SUBLANES, LANES)), e_ref)[0:1, :]
        xb = xs.astype(BF16)
        xw = (xs * wdec_x).astype(BF16)
        for g in range(SSD_GROUPS):
            cg = cs[:, g * SSD_STATE:(g + 1) * SSD_STATE]
            bg = bs[:, g * SSD_STATE:(g + 1) * SSD_STATE]
            cbm = lax.dot_general(cg, bg, (((1,), (1,)), ((), ())), preferred_element_type=F32)
            xg = xb[:, g * gw:(g + 1) * gw]
            y_g = None
            for r in range(hpg):
                hh = g * hpg + r
                seg = acs[:, hh:hh + 1] - acs_t[hh:hh + 1, :]
                lm = jnp.exp(jnp.where(causal, seg, -jnp.inf))
                w = (cbm * lm * dt_t[hh:hh + 1, :]).astype(BF16)
                part = _dot(w, jnp.where(glane == r, xg, jnp.zeros_like(xg)))
                y_g = part if y_g is None else y_g + part
            st = st_s[g]
            y_g = y_g + _dot(cg, st.astype(BF16)) * eacs_x[:, g * gw:(g + 1) * gw]
            st_s[g] = st * cd_x[:, g * gw:(g + 1) * gw] + lax.dot_general(
                bg, xw[:, g * gw:(g + 1) * gw], (((0,), (0,)), ((), ())), preferred_element_type=F32)
            y_s[r0:r0 + SSD_CHUNK, g * gw:(g + 1) * gw] = y_g

    z = _dot(u, wmain_ref[:, 2 * width:2 * width + inner])
    y = (y_s[...] + dexp_ref[...] * xbc[:, 0:inner]) * _silu(z)
    y_ssd = _group_rmsnorm(y, nw_ref[...], SSD_GROUPS)

    gates = _dot(u, wgate_ref[...])
    o_ref[0] = _merge_out(x, y_rg, y_ssd, gates, wprg_ref, wpssd_ref, wout_ref, npost_ref[...])

    @pl.when(i == nt - 1)
    def _():
        orgh_ref[0, pl.ds(b, 1), :] = hc_s[...]
        orgc_ref[0, 0] = cb_rg[pl.ds(SUBLANES - (CONV_W - 1), CONV_W - 1), :]
        osc_ref[0, 0] = cb_xbc[pl.ds(SUBLANES - (CONV_W - 1), CONV_W - 1), :]
        for g in range(SSD_GROUPS):
            ossd_ref[0, 0, pl.ds(g * gw, gw), :] = st_s[g].T


def _prompt_mixer_call(x, p, tm):
    nb, seq, d = x.shape
    width = p["rcw"].shape[1]
    inner = p["dexp"].shape[1]
    gn = SSD_GROUPS * SSD_STATE
    conv_dim = inner + 2 * gn
    assert seq % tm == 0 and tm % SSD_CHUNK == 0
    names = ["npre", "npost", "wmain", "wdt", "wgate", "rcw", "rcb", "wcat", "ba", "bx", "lam",
             "scw", "scb", "dtb", "alog", "dexp", "nw", "e", "wprg", "wpssd", "wout"]
    consts = [p[n] for n in names]
    out_shapes = (
        jax.ShapeDtypeStruct((nb, seq, d), F32),
        jax.ShapeDtypeStruct((1, nb, width), F32),
        jax.ShapeDtypeStruct((1, nb, CONV_W - 1, width), F32),
        jax.ShapeDtypeStruct((1, nb, inner, SSD_STATE), F32),
        jax.ShapeDtypeStruct((1, nb, CONV_W - 1, conv_dim), F32),
    )
    out_specs = (
        pl.BlockSpec((1, tm, d), lambda b, i: (b, i, 0)),
        pl.BlockSpec((1, nb, width), lambda b, i: (0, 0, 0)),
        pl.BlockSpec((1, 1, CONV_W - 1, width), lambda b, i: (0, b, 0, 0)),
        pl.BlockSpec((1, 1, inner, SSD_STATE), lambda b, i: (0, b, 0, 0)),
        pl.BlockSpec((1, 1, CONV_W - 1, conv_dim), lambda b, i: (0, b, 0, 0)),
    )
    scratch = [
        pltpu.VMEM((tm + SUBLANES, width), F32),
        pltpu.VMEM((tm + SUBLANES, conv_dim), F32),
        pltpu.VMEM((tm, width), F32),
        pltpu.VMEM((tm, width), F32),
        pltpu.VMEM((tm, width), F32),
        pltpu.VMEM((1, width), F32),
        pltpu.VMEM((tm, inner), F32),
        pltpu.VMEM((SSD_GROUPS, SSD_STATE, inner // SSD_GROUPS), F32),
    ]
    body = functools.partial(_prompt_mixer_body, tm=tm, width=width, inner=inner)
    return pl.pallas_call(
        body,
        grid=(nb, seq // tm),
        in_specs=[pl.BlockSpec((1, tm, d), lambda b, i: (b, i, 0))] + [_const_spec(c.shape) for c in consts],
        out_specs=out_specs,
        out_shape=out_shapes,
        scratch_shapes=scratch,
        compiler_params=pltpu.CompilerParams(dimension_semantics=("arbitrary", "arbitrary"),
                                             vmem_limit_bytes=VMEM_LIMIT_BYTES),
        name="prompt_mixer",
    )(x, *consts)


def _conv_steps(x, prev, w_ref, b_ref, nseq):
    steps = x.shape[0] // nseq
    xc = jnp.concatenate([prev, x], axis=0)
    outs = []
    for t in range(steps):
        y = b_ref[...]
        for k in range(CONV_W):
            y = y + xc[(t + k) * nseq:(t + k + 1) * nseq, :] * w_ref[k:k + 1, :]
        outs.append(y)
    return jnp.concatenate(outs, axis=0), xc[steps * nseq:, :]


def _sample_front_body(x_ref, rgc_ref, rgh_ref, sc_ref,
                       npre_ref, wmain_ref, wdt_ref, wgate_ref, rcw_ref, rcb_ref, wcat_ref, ba_ref, bx_ref,
                       lam_ref, scw_ref, scb_ref, dtb_ref, alog_ref, dexp_ref, e_ref, gsum_ref, wprg_ref,
                       mrg_o, rgh_o, rgc_o, sc_o, c_o, b_o, xw_o, cd_o, yp_o, ea_o, sz_o, sg_o,
                       *, nseq, steps, width, inner):
    m = steps * nseq
    gn = SSD_GROUPS * SSD_STATE
    heads = inner // SSD_HEAD_DIM
    x = x_ref[...].reshape(m, x_ref.shape[-1])
    u = _rms(x, npre_ref[...]).astype(BF16)

    rg_x = _dot(u, wmain_ref[:, 0:width])
    xc, new_rgc = _conv_steps(rg_x, rgc_ref[...].reshape((CONV_W - 1) * nseq, width), rcw_ref, rcb_ref, nseq)
    rgc_o[...] = new_rgc.reshape(CONV_W - 1, nseq, width)
    log_a, gate_x = _rg_gates(xc, wcat_ref, ba_ref[...], bx_ref[...], lam_ref[...])
    a = jnp.exp(log_a)
    uu = xc * gate_x * jnp.sqrt(_neg_expm1_2x(log_a, a))
    h = rgh_ref[...]
    hs = []
    for t in range(steps):
        h = a[t * nseq:(t + 1) * nseq, :] * h + uu[t * nseq:(t + 1) * nseq, :]
        hs.append(h)
    rgh_o[...] = h
    y_rg = jnp.concatenate(hs, axis=0) * _gelu_tanh(_dot(u, wmain_ref[:, width:2 * width]))
    gates = _dot(u, wgate_ref[...])
    mrg_o[...] = (_sigmoid(gates[:, :width]) * _dot(y_rg.astype(BF16), wprg_ref[...])).reshape(steps, nseq, width)
    sg_o[...] = _sigmoid(gates[:, width:]).reshape(steps, nseq, width)

    xbc = _dot(u, wmain_ref[:, 2 * width + inner:2 * width + 2 * inner + 2 * gn])
    xbc, new_sc = _conv_steps(xbc, sc_ref[...].reshape((CONV_W - 1) * nseq, inner + 2 * gn), scw_ref, scb_ref, nseq)
    sc_o[...] = new_sc.reshape(CONV_W - 1, nseq, inner + 2 * gn)
    xbc = _silu(xbc)
    xs = xbc[:, 0:inner]
    bs = xbc[:, inner:inner + gn]
    cs = xbc[:, inner + gn:inner + 2 * gn]
    b_o[...] = bs.reshape(steps, nseq, gn)
    c_o[...] = cs.reshape(steps, nseq, gn)
    lane = lax.broadcasted_iota(jnp.int32, (1, LANES), 1)
    dt = jnp.where(lane < heads, _softplus(_dot(u, wdt_ref[...]) + dtb_ref[...]), 0.0)
    da = dt * (-jnp.exp(alog_ref[...]))
    sl = lambda v, t: v[t * nseq:(t + 1) * nseq, :]
    acs = []
    run = None
    for t in range(steps):
        run = sl(da, t) if run is None else run + sl(da, t)
        acs.append(run)
    last = acs[-1]
    gsum = gsum_ref[...]
    for t in range(steps):
        acc = dexp_ref[...] * sl(xs, t)
        for j in range(t + 1):
            hi, mid, lo = _split3(sl(cs, t) * sl(bs, j))
            cb = _dot(hi, gsum) + _dot(mid, gsum) + _dot(lo, gsum)
            w = cb * jnp.exp(acs[t] - acs[j]) * sl(dt, j)
            acc = acc + _expand_heads(w, e_ref) * sl(xs, j)
        yp_o[t] = acc
        ea_o[t] = _expand_heads(jnp.exp(acs[t]), e_ref)
        xw_o[t] = sl(xs, t) * _expand_heads(jnp.exp(last - acs[t]) * sl(dt, t), e_ref)
    cd_o[...] = _expand_heads(jnp.exp(last), e_ref)
    sz_o[...] = _silu(_dot(u, wmain_ref[:, 2 * width:2 * width + inner])).reshape(steps, nseq, inner)


def _sample_front_call(x3, rgc3, rgh, sc3, p, nseq_blk):
    steps, ns, d = x3.shape
    width = p["rcw"].shape[1]
    inner = p["dexp"].shape[1]
    gn = SSD_GROUPS * SSD_STATE
    conv_dim = inner + 2 * gn
    assert ns % nseq_blk == 0
    names = ["npre", "wmain", "wdt", "wgate", "rcw", "rcb", "wcat", "ba", "bx", "lam",
             "scw", "scb", "dtb", "alog", "dexp", "e", "gsum", "wprg"]
    consts = [p[n] for n in names]
    blk3 = lambda k, n: pl.BlockSpec((k, nseq_blk, n), lambda i: (0, i, 0))
    blk2 = lambda n: pl.BlockSpec((nseq_blk, n), lambda i: (i, 0))
    sds = jax.ShapeDtypeStruct
    out_shape = (sds((steps, ns, width), F32), sds((ns, width), F32), sds((CONV_W - 1, ns, width), F32),
                 sds((CONV_W - 1, ns, conv_dim), F32), sds((steps, ns, gn), F32), sds((steps, ns, gn), F32),
                 sds((steps, ns, inner), F32), sds((ns, inner), F32), sds((steps, ns, inner), F32),
                 sds((steps, ns, inner), F32), sds((steps, ns, inner), F32), sds((steps, ns, width), F32))
    out_specs = (blk3(steps, width), blk2(width), blk3(CONV_W - 1, width), blk3(CONV_W - 1, conv_dim),
                 blk3(steps, gn), blk3(steps, gn), blk3(steps, inner), blk2(inner), blk3(steps, inner),
                 blk3(steps, inner), blk3(steps, inner), blk3(steps, width))
    body = functools.partial(_sample_front_body, nseq=nseq_blk, steps=steps, width=width, inner=inner)
    return pl.pallas_call(
        body,
        grid=(ns // nseq_blk,),
        in_specs=[blk3(steps, d), blk3(CONV_W - 1, width), blk2(width), blk3(CONV_W - 1, conv_dim)]
        + [_const_spec(c.shape) for c in consts],
        out_specs=out_specs,
        out_shape=out_shape,
        compiler_params=pltpu.CompilerParams(dimension_semantics=("arbitrary",),
                                             vmem_limit_bytes=VMEM_LIMIT_BYTES),
        name="sample_front",
    )(x3, rgc3, rgh, sc3, *consts)


def _sample_state_body(s_ref, c_ref, b_ref, xw_ref, cd_ref, so_ref, yoff_ref, *, nseq, steps, inner):
    m = steps * nseq
    gw = inner // SSD_GROUPS
    cb = c_ref[...].reshape(m, c_ref.shape[-1]).astype(BF16)
    bb = b_ref[...].reshape(m, b_ref.shape[-1]).astype(BF16)
    xw = xw_ref[...].reshape(m, inner).astype(BF16)
    rowseq = lax.broadcasted_iota(jnp.int32, (m, 1), 0) % nseq
    pad = jnp.zeros((LANES - nseq, gw), F32)
    for g in range(SSD_GROUPS):
        cg = cb[:, g * SSD_STATE:(g + 1) * SSD_STATE]
        bg = bb[:, g * SSD_STATE:(g + 1) * SSD_STATE]
        xg = xw[:, g * gw:(g + 1) * gw]
        cd_t = jnp.concatenate([cd_ref[:, g * gw:(g + 1) * gw], pad], axis=0).T
        yoff = jnp.zeros((m, gw), F32)
        for q in range(nseq):
            mine = rowseq == q
            s0 = s_ref[q, pl.ds(g * gw, gw), :]
            res = lax.dot_general(cg, s0.astype(BF16), (((1,), (1,)), ((), ())), preferred_element_type=F32)
            yoff = yoff + jnp.where(mine, res, 0.0)
            xq = jnp.where(mine, xg, jnp.zeros_like(xg))
            sadd = lax.dot_general(xq, bg, (((0,), (0,)), ((), ())), preferred_element_type=F32)
            so_ref[q, pl.ds(g * gw, gw), :] = s0 * cd_t[:, q:q + 1] + sadd
        yoff_ref[:, :, pl.ds(g * gw, gw)] = yoff.reshape(steps, nseq, gw)


def _sample_state_call(state, c3, b3, xw3, cd, nseq_blk):
    ns, inner, n = state.shape
    steps = c3.shape[0]
    gn = c3.shape[-1]
    assert ns % nseq_blk == 0 and nseq_blk % SUBLANES == 0
    blk3 = lambda w: pl.BlockSpec((steps, nseq_blk, w), lambda i: (0, i, 0))
    sblk = pl.BlockSpec((nseq_blk, inner, n), lambda i: (i, 0, 0))
    body = functools.partial(_sample_state_body, nseq=nseq_blk, steps=steps, inner=inner)
    return pl.pallas_call(
        body,
        grid=(ns // nseq_blk,),
        in_specs=[sblk, blk3(gn), blk3(gn), blk3(inner), pl.BlockSpec((nseq_blk, inner), lambda i: (i, 0))],
        out_specs=(sblk, blk3(inner)),
        out_shape=(jax.ShapeDtypeStruct(state.shape, F32), jax.ShapeDtypeStruct((steps, ns, inner), F32)),
        compiler_params=pltpu.CompilerParams(dimension_semantics=("arbitrary",),
                                             vmem_limit_bytes=VMEM_LIMIT_BYTES),
        name="sample_state",
    )(state, c3, b3, xw3, cd)


def _sample_back_body(x_ref, mrg_ref, yp_ref, yoff_ref, ea_ref, sz_ref, sg_ref,
                      nw_ref, wpssd_ref, wout_ref, npost_ref, o_ref):
    y = (yp_ref[...] + yoff_ref[...] * ea_ref[...]) * sz_ref[...]
    y_ssd = _group_rmsnorm(y, nw_ref[...], SSD_GROUPS)
    m = mrg_ref[...] + sg_ref[...] * _dot(y_ssd.astype(BF16), wpssd_ref[...])
    out = _dot(m.astype(BF16), wout_ref[...])
    o_ref[...] = x_ref[...] + _rms(out, npost_ref[...])


def _sample_back_call(x2, mrg, yp, yoff, ea, sz, sg, p, tm):
    t, d = x2.shape
    assert t % tm == 0
    consts = [p[n] for n in ("nw", "wpssd", "wout", "npost")]
    rows = lambda a: pl.BlockSpec((tm, a.shape[1]), lambda i: (i, 0))
    acts = [x2, mrg, yp, yoff, ea, sz, sg]
    return pl.pallas_call(
        _sample_back_body,
        grid=(t // tm,),
        in_specs=[rows(a) for a in acts] + [_const_spec(c.shape) for c in consts],
        out_specs=pl.BlockSpec((tm, d), lambda i: (i, 0)),
        out_shape=jax.ShapeDtypeStruct((t, d), F32),
        compiler_params=pltpu.CompilerParams(dimension_semantics=("arbitrary",),
                                             vmem_limit_bytes=VMEM_LIMIT_BYTES),
        name="sample_back",
    )(*acts, *consts)


def _prep_mixer_params(n_mix_pre, n_mix_post, w_in, rg_conv_w, rg_conv_b, rg_wa, rg_ba, rg_wx, rg_bx,
                       rg_lambda, ssd_conv_w, ssd_conv_b, ssd_dt_bias, ssd_a_log, ssd_d, ssd_norm_w,
                       w_proj_rg, w_proj_ssd, w_out):
    width = rg_conv_w.shape[1]
    heads = ssd_dt_bias.shape[0]
    inner = ssd_norm_w.shape[0]
    conv_dim = ssd_conv_w.shape[1]
    main = 2 * width + inner + conv_dim
    row = lambda v: v.reshape(1, -1).astype(F32)
    padl = lambda v: jnp.pad(v.reshape(1, -1).astype(F32), ((0, 0), (0, LANES - v.shape[-1])))
    e = (jnp.arange(LANES)[:, None] == (jnp.arange(inner)[None, :] // SSD_HEAD_DIM)).astype(BF16)
    head_group = jnp.where(jnp.arange(LANES) < heads, jnp.arange(LANES) // (heads // SSD_GROUPS), -1)
    gsum = ((jnp.arange(SSD_GROUPS * SSD_STATE)[:, None] // SSD_STATE) == head_group[None, :]).astype(BF16)
    return dict(
        gsum=gsum,
        npre=row(n_mix_pre), npost=row(n_mix_post),
        wmain=w_in[:, :main].astype(BF16),
        wdt=jnp.pad(w_in[:, main:main + heads], ((0, 0), (0, LANES - heads))).astype(BF16),
        wgate=w_in[:, main + heads:].astype(BF16),
        rcw=rg_conv_w.astype(F32), rcb=row(rg_conv_b),
        wcat=jnp.concatenate([rg_wa, rg_wx], axis=-1).astype(BF16),
        ba=row(rg_ba), bx=row(rg_bx), lam=row(rg_lambda),
        scw=ssd_conv_w.astype(F32), scb=row(ssd_conv_b),
        dtb=padl(ssd_dt_bias), alog=padl(ssd_a_log),
        dexp=row(jnp.repeat(ssd_d, SSD_HEAD_DIM)), nw=row(ssd_norm_w), e=e,
        wprg=w_proj_rg.astype(BF16), wpssd=w_proj_ssd.astype(BF16), wout=w_out.astype(BF16),
    )


FFN_TILE = 512
PROMPT_TILE = 256
SAMPLE_FRONT_SEQS = 32
SAMPLE_STATE_SEQS = 8
SAMPLE_BACK_TILE = 256


def _prompt_layer(x, mp, f1, f2):
    nb, seq, d = x.shape
    x = _ffn_call(x.reshape(nb * seq, d), *f1, min(FFN_TILE, nb * seq)).reshape(nb, seq, d)
    x, rgh, rgc, ssd, ssdc = _prompt_mixer_call(x, mp, min(PROMPT_TILE, seq))
    x = _ffn_call(x.reshape(nb * seq, d), *f2, min(FFN_TILE, nb * seq)).reshape(nb, seq, d)
    heads = ssd.shape[2] // SSD_HEAD_DIM
    return x, (rgh[0], rgc[0], ssd[0].reshape(nb, heads, SSD_HEAD_DIM, SSD_STATE), ssdc[0])


def _sample_layer(x, rg_h, rg_conv, ssd, ssd_conv, mp, f1, f2):
    ns, steps, d = x.shape
    heads, hd, n = ssd.shape[1:]
    t = steps * ns
    to_steps = lambda v: jnp.transpose(v, (1, 0, 2))
    flat = lambda v: v.reshape(t, v.shape[-1])
    x = _ffn_call(to_steps(x).reshape(t, d), *f1, min(FFN_TILE, t))
    (mrg, rgh_new, rgc_new, sc_new, c3, b3, xw3, cd, yp3, ea3, sz3, sg3) = _sample_front_call(
        x.reshape(steps, ns, d), to_steps(rg_conv), rg_h, to_steps(ssd_conv), mp, min(SAMPLE_FRONT_SEQS, ns))
    ssd_new, yoff3 = _sample_state_call(ssd.reshape(ns, heads * hd, n), c3, b3, xw3, cd, SAMPLE_STATE_SEQS)
    x = _sample_back_call(x, flat(mrg), flat(yp3), flat(yoff3), flat(ea3), flat(sz3), flat(sg3), mp,
                          min(SAMPLE_BACK_TILE, t))
    x = _ffn_call(x, *f2, min(FFN_TILE, t))
    return (to_steps(x.reshape(steps, ns, d)),
            (rgh_new, to_steps(rgc_new), ssd_new.reshape(ns, heads, hd, n), to_steps(sc_new)))


def kernel(x_prompt, x_sample, state_rg_h, state_rg_conv, state_ssd, state_ssd_conv, n_ffn1_pre, n_ffn1_post, ffn1_wg, ffn1_wu, ffn1_wd, n_mix_pre, n_mix_post, w_in, rg_conv_w, rg_conv_b, rg_wa, rg_ba, rg_wx, rg_bx, rg_lambda, ssd_conv_w, ssd_conv_b, ssd_dt_bias, ssd_a_log, ssd_d, ssd_norm_w, w_proj_rg, w_proj_ssd, w_out, n_ffn2_pre, n_ffn2_post, ffn2_wg, ffn2_wu, ffn2_wd):
    depth = w_in.shape[0]
    row = lambda v: v.reshape(1, -1).astype(F32)
    yp, ys = x_prompt, x_sample
    p_new = ([], [], [], [])
    s_new = ([], [], [], [])
    for li in range(depth):
        mp = _prep_mixer_params(n_mix_pre[li], n_mix_post[li], w_in[li], rg_conv_w[li], rg_conv_b[li], rg_wa[li],
                                rg_ba[li], rg_wx[li], rg_bx[li], rg_lambda[li], ssd_conv_w[li], ssd_conv_b[li],
                                ssd_dt_bias[li], ssd_a_log[li], ssd_d[li], ssd_norm_w[li], w_proj_rg[li],
                                w_proj_ssd[li], w_out[li])
        f1 = (row(n_ffn1_pre[li]), row(n_ffn1_post[li]), ffn1_wg[li].astype(BF16), ffn1_wu[li].astype(BF16),
              ffn1_wd[li].astype(BF16))
        f2 = (row(n_ffn2_pre[li]), row(n_ffn2_post[li]), ffn2_wg[li].astype(BF16), ffn2_wu[li].astype(BF16),
              ffn2_wd[li].astype(BF16))
        yp, newp = _prompt_layer(yp, mp, f1, f2)
        ys, news = _sample_layer(ys, state_rg_h[li], state_rg_conv[li], state_ssd[li], state_ssd_conv[li], mp, f1, f2)
        for lst, v in zip(p_new, newp):
            lst.append(v)
        for lst, v in zip(s_new, news):
            lst.append(v)
    prompt_state = [jnp.stack(v, 0) for v in p_new]
    sample_state = [jnp.stack(v, 0) for v in s_new]
    return (yp, ys, *prompt_state, *sample_state)
```

```python
import functools

import jax
import jax.numpy as jnp
from jax import lax
from jax.experimental import pallas as pl
from jax.experimental.pallas import tpu as pltpu

F32 = jnp.float32
BF16 = jnp.bfloat16

EPS = 1e-6
RG_C = 8.0
CONV_W = 4
RG_BLOCKS = 8
SSD_HEAD_DIM = 64
SSD_GROUPS = 8
SSD_STATE = 128
SSD_CHUNK = 128
LANES = 128
SUBLANES = 8
VMEM_LIMIT_BYTES = 60 * 1024 * 1024


def _dot(a, b):
    return jnp.dot(a, b, preferred_element_type=F32)


def _rms(x, w):
    return x * lax.rsqrt(jnp.mean(x * x, axis=-1, keepdims=True) + EPS) * w


def _sigmoid(x):
    return 1.0 / (1.0 + jnp.exp(-x))


def _silu(x):
    return x * _sigmoid(x)


def _softplus(x):
    return jnp.maximum(x, 0.0) + jnp.log1p(jnp.exp(-jnp.abs(x)))


def _neg_expm1_2x(x, ex):
    e2 = ex * ex
    return jnp.where(jnp.abs(x) > 0.25, 1.0 - e2, -jnp.tanh(x) * (e2 + 1.0))


def _gelu_tanh(x):
    return 0.5 * x * (1.0 + jnp.tanh(0.7978845608028654 * (x + 0.044715 * (x * x * x))))


def _split3(v):
    hi = v.astype(BF16)
    r1 = v - hi.astype(F32)
    mid = r1.astype(BF16)
    lo = (r1 - mid.astype(F32)).astype(BF16)
    return hi, mid, lo


def _expand_heads(v, e_ref):
    e = e_ref[...]
    hi, mid, lo = _split3(v)
    return _dot(hi, e) + _dot(mid, e) + _dot(lo, e)


def _const_spec(shape):
    nd = len(shape)
    return pl.BlockSpec(shape, lambda *_: (0,) * nd, pipeline_mode=pl.Buffered(1))


def _ffn_body(x_ref, pre_ref, post_ref, wg_ref, wu_ref, wd_ref, o_ref):
    x = x_ref[...]
    u = _rms(x, pre_ref[...]).astype(BF16)
    g = _dot(u, wg_ref[...])
    up = _dot(u, wu_ref[...])
    h = (_silu(g) * up).astype(BF16)
    y = _dot(h, wd_ref[...])
    o_ref[...] = x + 0.5 * _rms(y, post_ref[...])


def _ffn_call(x2d, pre, post, wg, wu, wd, tm):
    t, d = x2d.shape
    assert t % tm == 0
    return pl.pallas_call(
        _ffn_body,
        grid=(t // tm,),
        in_specs=[pl.BlockSpec((tm, d), lambda i: (i, 0)),
                  _const_spec(pre.shape), _const_spec(post.shape),
                  _const_spec(wg.shape), _const_spec(wu.shape), _const_spec(wd.shape)],
        out_specs=pl.BlockSpec((tm, d), lambda i: (i, 0)),
        out_shape=jax.ShapeDtypeStruct((t, d), F32),
        compiler_params=pltpu.CompilerParams(dimension_semantics=("arbitrary",),
                                             vmem_limit_bytes=VMEM_LIMIT_BYTES),
        name="ffn_half_step",
    )(x2d, pre, post, wg, wu, wd)


def _rg_gates(xc, wcat_ref, ba, bx, lam):
    blk = xc.shape[1] // RG_BLOCKS
    xb = xc.astype(BF16)
    ga, gx = [], []
    for h in range(RG_BLOCKS):
        r = _dot(xb[:, h * blk:(h + 1) * blk], wcat_ref[h])
        ga.append(r[:, :blk])
        gx.append(r[:, blk:])
    gate_a = _sigmoid(jnp.concatenate(ga, axis=1) + ba)
    gate_x = _sigmoid(jnp.concatenate(gx, axis=1) + bx)
    log_a = (-RG_C) * gate_a * _softplus(-lam)
    return log_a, gate_x


def _group_rmsnorm(y, nw, groups):
    gw = y.shape[1] // groups
    outs = []
    for g in range(groups):
        yg = y[:, g * gw:(g + 1) * gw]
        outs.append(yg * lax.rsqrt(jnp.mean(yg * yg, axis=-1, keepdims=True) + EPS))
    return jnp.concatenate(outs, axis=1) * nw


PIECE = 512
ROWS = 32
RG_ROWS = 64


def _conv_rows(cbuf_ref, r0, nrows, cols, w_ref, b_ref):
    y = b_ref[:, cols]
    for s in range(CONV_W):
        y = y + cbuf_ref[pl.ds(SUBLANES + r0 - s, nrows), cols] * w_ref[CONV_W - 1 - s:CONV_W - s, cols]
    return y


def _prompt_mixer_body(x_ref, npre_ref, wmain_ref, wdt_ref,
                       rcw_ref, rcb_ref, wcat_ref, ba_ref, bx_ref, lam_ref,
                       scw_ref, scb_ref, dtb_ref, alog_ref, dexp_ref, nw_ref, e_ref,
                       oyrg_ref, oyssd_ref, orgh_ref, orgc_ref, ossd_ref, osc_ref,
                       cb_rg, cb_xbc, a_s, u_s, hc_s, y_s, st_s, sz_s, gg_s, ub_s,
                       *, tm, width, inner):
    b = pl.program_id(0)
    i = pl.program_id(1)
    nt = pl.num_programs(1)
    nch = tm // SSD_CHUNK
    gn = SSD_GROUPS * SSD_STATE
    heads = inner // SSD_HEAD_DIM
    gw = inner // SSD_GROUPS
    hpg = heads // SSD_GROUPS

    @pl.when(i == 0)
    def _():
        cb_rg[pl.ds(0, SUBLANES), :] = jnp.zeros((SUBLANES, width), F32)
        cb_xbc[pl.ds(0, SUBLANES), :] = jnp.zeros((SUBLANES, inner + 2 * gn), F32)
        hc_s[...] = jnp.zeros_like(hc_s)
        st_s[...] = jnp.zeros_like(st_s)

    for r0 in range(0, tm, ROWS):
        ub_s[r0:r0 + ROWS, :] = _rms(x_ref[0, r0:r0 + ROWS, :], npre_ref[...]).astype(BF16)
    u = ub_s[...]

    lane = lax.broadcasted_iota(jnp.int32, (1, LANES), 1)
    dt_all = jnp.where(lane < heads, _softplus(_dot(u, wdt_ref[...]) + dtb_ref[...]), 0.0)
    da_all = dt_all * (-jnp.exp(alog_ref[...]))
    ri = lax.broadcasted_iota(jnp.int32, (tm, tm), 0)
    ci = lax.broadcasted_iota(jnp.int32, (tm, tm), 1)
    chunk_tri = jnp.logical_and(ri >= ci, ri // SSD_CHUNK == ci // SSD_CHUNK).astype(F32)
    acs_all = jnp.dot(chunk_tri, da_all, preferred_element_type=F32, precision=lax.Precision.HIGHEST)
    acs_t_all = acs_all.T
    dt_t_all = dt_all.T
    eacs_all = jnp.exp(acs_all)
    lasts = [acs_all[(c + 1) * SSD_CHUNK - 1:(c + 1) * SSD_CHUNK, :] for c in range(nch)]
    cd_rows = jnp.concatenate([jnp.exp(v) for v in lasts] + [jnp.zeros((SUBLANES - nch, LANES), F32)], axis=0)
    cd_x_all = _expand_heads(cd_rows, e_ref)
    causal = (lax.broadcasted_iota(jnp.int32, (SSD_CHUNK, SSD_CHUNK), 0)
              >= lax.broadcasted_iota(jnp.int32, (SSD_CHUNK, SSD_CHUNK), 1))
    glane = lax.broadcasted_iota(jnp.int32, (1, gw), 1) // SSD_HEAD_DIM

    body = pl.ds(SUBLANES, tm)
    xbc0 = 2 * width + inner
    cdim = inner + 2 * gn
    n_xbc = cdim // PIECE
    blk = width // RG_BLOCKS

    for p in range(width // PIECE):
        cols = slice(p * PIECE, (p + 1) * PIECE)
        cb_rg[body, cols] = _dot(u, wmain_ref[:, cols])

    for h in range(RG_BLOCKS):
        for pc in range(h * n_xbc // RG_BLOCKS, (h + 1) * n_xbc // RG_BLOCKS):
            cb_xbc[body, pc * PIECE:(pc + 1) * PIECE] = _dot(
                u, wmain_ref[:, xbc0 + pc * PIECE:xbc0 + (pc + 1) * PIECE])
        cols = slice(h * blk, (h + 1) * blk)
        for r0 in range(0, tm, RG_ROWS):
            u_s[r0:r0 + RG_ROWS, cols] = _conv_rows(cb_rg, r0, RG_ROWS, cols, rcw_ref, rcb_ref)
        gates_ax = _dot(u_s[:, cols].astype(BF16), wcat_ref[h])
        sp = _softplus(-lam_ref[:, cols])
        for r0 in range(0, tm, RG_ROWS):
            gate_a = _sigmoid(gates_ax[r0:r0 + RG_ROWS, :blk] + ba_ref[:, cols])
            gate_x = _sigmoid(gates_ax[r0:r0 + RG_ROWS, blk:] + bx_ref[:, cols])
            log_a = (-RG_C) * gate_a * sp
            a = jnp.exp(log_a)
            mult = jnp.sqrt(_neg_expm1_2x(log_a, a))
            if r0 == 0:
                first = jnp.logical_and(i == 0, lax.broadcasted_iota(jnp.int32, (RG_ROWS, 1), 0) == 0)
                a = jnp.where(first, 0.0, a)
                mult = jnp.where(first, 1.0, mult)
            a_s[r0:r0 + RG_ROWS, cols] = a
            u_s[r0:r0 + RG_ROWS, cols] = u_s[r0:r0 + RG_ROWS, cols] * gate_x * mult
    cb_rg[pl.ds(0, SUBLANES), :] = cb_rg[pl.ds(tm, SUBLANES), :]

    side = ([(sz_s, p, 2 * width + p * PIECE, _silu) for p in range(inner // PIECE)]
            + [(gg_s, p, width + p * PIECE, _gelu_tanh) for p in range(width // PIECE)])
    for pc in range(n_xbc):
        for dst, p, w0, act in side[pc * len(side) // n_xbc:(pc + 1) * len(side) // n_xbc]:
            proj = _dot(u, wmain_ref[:, w0:w0 + PIECE])
            for r0 in range(0, tm, ROWS):
                dst[r0:r0 + ROWS, p * PIECE:(p + 1) * PIECE] = act(proj[r0:r0 + ROWS, :])
        cols = slice(pc * PIECE, (pc + 1) * PIECE)
        tail = cb_xbc[pl.ds(tm, SUBLANES), cols]
        for r0 in reversed(range(0, tm, ROWS)):
            cb_xbc[pl.ds(SUBLANES + r0, ROWS), cols] = _silu(_conv_rows(cb_xbc, r0, ROWS, cols, scw_ref, scb_ref))
        cb_xbc[pl.ds(0, SUBLANES), cols] = tail

    def cb_product(c, g):
        rows = pl.ds(SUBLANES + c * SSD_CHUNK, SSD_CHUNK)
        cg = cb_xbc[rows, pl.ds(inner + gn + g * SSD_STATE, SSD_STATE)]
        bg = cb_xbc[rows, pl.ds(inner + g * SSD_STATE, SSD_STATE)].astype(BF16)
        return cg, bg, lax.dot_general(cg.astype(BF16), bg, (((1,), (1,)), ((), ())), preferred_element_type=F32)

    def consume(c, g, res):
        y_g = res[0:SSD_CHUNK, :]
        for r in range(1, hpg):
            y_g = jnp.where(glane == r, res[r * SSD_CHUNK:(r + 1) * SSD_CHUNK, :], y_g)
        y_s[c * SSD_CHUNK:(c + 1) * SSD_CHUNK, g * gw:(g + 1) * gw] = y_g

    wdecs = [jnp.exp(lasts[c] - acs_all[c * SSD_CHUNK:(c + 1) * SSD_CHUNK, :]) * dt_all[c * SSD_CHUNK:(c + 1) * SSD_CHUNK, :]
             for c in range(nch)]
    items = [(c, g) for c in range(nch) for g in range(SSD_GROUPS)]
    nxt = cb_product(*items[0])
    pending = None
    for k, (c, g) in enumerate(items):
        cg, bg, cbm = nxt
        if k + 1 < len(items):
            nxt = cb_product(*items[k + 1])
        r0 = c * SSD_CHUNK
        rows = pl.ds(SUBLANES + r0, SSD_CHUNK)
        acs = acs_all[r0:r0 + SSD_CHUNK, :]
        xg = cb_xbc[rows, pl.ds(g * gw, gw)]
        st = st_s[g]
        rhs = jnp.concatenate([xg.astype(BF16), st.astype(BF16)], axis=0)
        lhs, scale = [], None
        for r in range(hpg):
            hh = g * hpg + r
            seg = acs[:, hh:hh + 1] - acs_t_all[hh:hh + 1, r0:r0 + SSD_CHUNK]
            lm = jnp.exp(jnp.where(causal, seg, -jnp.inf))
            w = (cbm * lm * dt_t_all[hh:hh + 1, r0:r0 + SSD_CHUNK]).astype(BF16)
            ce = (cg * eacs_all[r0:r0 + SSD_CHUNK, hh:hh + 1]).astype(BF16)
            lhs.append(jnp.concatenate([w, ce], axis=1))
            col = jnp.broadcast_to(wdecs[c][:, hh:hh + 1], (SSD_CHUNK, gw))
            scale = col if scale is None else jnp.where(glane == r, col, scale)
        res = _dot(jnp.concatenate(lhs, axis=0), rhs)
        if pending is not None:
            consume(*pending)
        pending = (c, g, res)
        st_s[g] = st * cd_x_all[c:c + 1, g * gw:(g + 1) * gw] + lax.dot_general(
            bg, (xg * scale).astype(BF16), (((0,), (0,)), ((), ())), preferred_element_type=F32)
    consume(*pending)

    for g in range(SSD_GROUPS):
        cols = slice(g * gw, (g + 1) * gw)
        for r0 in range(0, tm, RG_ROWS):
            rws = slice(r0, r0 + RG_ROWS)
            yg = (y_s[rws, cols] + dexp_ref[:, cols] * cb_xbc[pl.ds(SUBLANES + r0, RG_ROWS), cols]) * sz_s[rws, cols]
            yg = yg * lax.rsqrt(jnp.mean(yg * yg, axis=-1, keepdims=True) + EPS) * nw_ref[:, cols]
            oyssd_ref[0, rws, cols] = yg.astype(BF16)

    def scan_step(t, h):
        h = a_s[pl.ds(t, 1), :] * h + u_s[pl.ds(t, 1), :]
        u_s[pl.ds(t, 1), :] = h
        return h

    hc_s[...] = lax.fori_loop(0, tm, scan_step, hc_s[...], unroll=8)
    for r0 in range(0, tm, ROWS):
        oyrg_ref[0, r0:r0 + ROWS, :] = (u_s[r0:r0 + ROWS, :] * gg_s[r0:r0 + ROWS, :]).astype(BF16)

    @pl.when(i == nt - 1)
    def _():
        orgh_ref[0, pl.ds(b, 1), :] = hc_s[...]
        orgc_ref[0, 0] = cb_rg[pl.ds(SUBLANES - (CONV_W - 1), CONV_W - 1), :]
        osc_ref[0, 0] = cb_xbc[pl.ds(SUBLANES - (CONV_W - 1), CONV_W - 1), :]
        for g in range(SSD_GROUPS):
            ossd_ref[0, 0, pl.ds(g * gw, gw), :] = st_s[g].T


def _prompt_mixer_call(x, p, tm):
    nb, seq, d = x.shape
    width = p["rcw"].shape[1]
    inner = p["dexp"].shape[1]
    gn = SSD_GROUPS * SSD_STATE
    conv_dim = inner + 2 * gn
    assert seq % tm == 0 and tm % SSD_CHUNK == 0
    names = ["npre", "wmain", "wdt", "rcw", "rcb", "wcat", "ba", "bx", "lam",
             "scw", "scb", "dtb", "alog", "dexp", "nw", "e"]
    consts = [p[n] for n in names]
    out_shapes = (
        jax.ShapeDtypeStruct((nb, seq, width), BF16),
        jax.ShapeDtypeStruct((nb, seq, inner), BF16),
        jax.ShapeDtypeStruct((1, nb, width), F32),
        jax.ShapeDtypeStruct((1, nb, CONV_W - 1, width), F32),
        jax.ShapeDtypeStruct((1, nb, inner, SSD_STATE), F32),
        jax.ShapeDtypeStruct((1, nb, CONV_W - 1, conv_dim), F32),
    )
    out_specs = (
        pl.BlockSpec((1, tm, width), lambda b, i: (b, i, 0)),
        pl.BlockSpec((1, tm, inner), lambda b, i: (b, i, 0)),
        pl.BlockSpec((1, nb, width), lambda b, i: (0, 0, 0)),
        pl.BlockSpec((1, 1, CONV_W - 1, width), lambda b, i: (0, b, 0, 0)),
        pl.BlockSpec((1, 1, inner, SSD_STATE), lambda b, i: (0, b, 0, 0), pipeline_mode=pl.Buffered(1)),
        pl.BlockSpec((1, 1, CONV_W - 1, conv_dim), lambda b, i: (0, b, 0, 0)),
    )
    scratch = [
        pltpu.VMEM((tm + SUBLANES, width), F32),
        pltpu.VMEM((tm + SUBLANES, conv_dim), F32),
        pltpu.VMEM((tm, width), F32),
        pltpu.VMEM((tm, width), F32),
        pltpu.VMEM((1, width), F32),
        pltpu.VMEM((tm, inner), F32),
        pltpu.VMEM((SSD_GROUPS, SSD_STATE, inner // SSD_GROUPS), F32),
        pltpu.VMEM((tm, inner), F32),
        pltpu.VMEM((tm, width), F32),
        pltpu.VMEM((tm, d), BF16),
    ]
    body = functools.partial(_prompt_mixer_body, tm=tm, width=width, inner=inner)
    return pl.pallas_call(
        body,
        grid=(nb, seq // tm),
        in_specs=[pl.BlockSpec((1, tm, d), lambda b, i: (b, i, 0))] + [_const_spec(c.shape) for c in consts],
        out_specs=out_specs,
        out_shape=out_shapes,
        scratch_shapes=scratch,
        compiler_params=pltpu.CompilerParams(dimension_semantics=("arbitrary", "arbitrary"),
                                             vmem_limit_bytes=VMEM_LIMIT_BYTES),
        name="prompt_mixer",
    )(x, *consts)


def _merge_body(x_ref, yrg_ref, yssd_ref, npre_ref, npost_ref, wgate_ref, wprg_ref, wpssd_ref, wout_ref, o_ref):
    x = x_ref[...]
    width = yrg_ref.shape[1]
    u = _rms(x, npre_ref[...]).astype(BF16)
    m = (_sigmoid(_dot(u, wgate_ref[:, 0:width])) * _dot(yrg_ref[...], wprg_ref[...])
         + _sigmoid(_dot(u, wgate_ref[:, width:2 * width])) * _dot(yssd_ref[...], wpssd_ref[...]))
    out = _dot(m.astype(BF16), wout_ref[...])
    o_ref[...] = x + _rms(out, npost_ref[...])


def _merge_call(x2d, yrg, yssd, p, tm):
    t, d = x2d.shape
    assert t % tm == 0
    consts = [p[n] for n in ("npre", "npost", "wgate", "wprg", "wpssd", "wout")]
    rows = lambda a: pl.BlockSpec((tm, a.shape[1]), lambda i: (i, 0))
    return pl.pallas_call(
        _merge_body,
        grid=(t // tm,),
        in_specs=[rows(x2d), rows(yrg), rows(yssd)] + [_const_spec(c.shape) for c in consts],
        out_specs=pl.BlockSpec((tm, d), lambda i: (i, 0)),
        out_shape=jax.ShapeDtypeStruct((t, d), F32),
        compiler_params=pltpu.CompilerParams(dimension_semantics=("arbitrary",),
                                             vmem_limit_bytes=VMEM_LIMIT_BYTES),
        name="mixer_merge",
    )(x2d, yrg, yssd, *consts)


def _conv_steps(x, prev, w_ref, b_ref, nseq):
    steps = x.shape[0] // nseq
    xc = jnp.concatenate([prev, x], axis=0)
    outs = []
    for t in range(steps):
        y = b_ref[...]
        for k in range(CONV_W):
            y = y + xc[(t + k) * nseq:(t + k + 1) * nseq, :] * w_ref[k:k + 1, :]
        outs.append(y)
    return jnp.concatenate(outs, axis=0), xc[steps * nseq:, :]


def _sample_front_body(x_ref, rgc_ref, rgh_ref, sc_ref,
                       npre_ref, wmain_ref, wdt_ref, wgate_ref, rcw_ref, rcb_ref, wcat_ref, ba_ref, bx_ref,
                       lam_ref, scw_ref, scb_ref, dtb_ref, alog_ref, dexp_ref, e_ref, gsum_ref, wprg_ref,
                       mrg_o, rgh_o, rgc_o, sc_o, c_o, b_o, xw_o, cd_o, yp_o, ea_o, sz_o, sg_o,
                       *, nseq, steps, width, inner):
    m = steps * nseq
    gn = SSD_GROUPS * SSD_STATE
    heads = inner // SSD_HEAD_DIM
    x = x_ref[...].reshape(m, x_ref.shape[-1])
    u = _rms(x, npre_ref[...]).astype(BF16)

    rg_x = _dot(u, wmain_ref[:, 0:width])
    xc, new_rgc = _conv_steps(rg_x, rgc_ref[...].reshape((CONV_W - 1) * nseq, width), rcw_ref, rcb_ref, nseq)
    rgc_o[...] = new_rgc.reshape(CONV_W - 1, nseq, width)
    log_a, gate_x = _rg_gates(xc, wcat_ref, ba_ref[...], bx_ref[...], lam_ref[...])
    a = jnp.exp(log_a)
    uu = xc * gate_x * jnp.sqrt(_neg_expm1_2x(log_a, a))
    h = rgh_ref[...]
    hs = []
    for t in range(steps):
        h = a[t * nseq:(t + 1) * nseq, :] * h + uu[t * nseq:(t + 1) * nseq, :]
        hs.append(h)
    rgh_o[...] = h
    y_rg = jnp.concatenate(hs, axis=0) * _gelu_tanh(_dot(u, wmain_ref[:, width:2 * width]))
    gates = _dot(u, wgate_ref[...])
    mrg_o[...] = (_sigmoid(gates[:, :width]) * _dot(y_rg.astype(BF16), wprg_ref[...])).reshape(steps, nseq, width)
    sg_o[...] = _sigmoid(gates[:, width:]).reshape(steps, nseq, width)

    xbc = _dot(u, wmain_ref[:, 2 * width + inner:2 * width + 2 * inner + 2 * gn])
    xbc, new_sc = _conv_steps(xbc, sc_ref[...].reshape((CONV_W - 1) * nseq, inner + 2 * gn), scw_ref, scb_ref, nseq)
    sc_o[...] = new_sc.reshape(CONV_W - 1, nseq, inner + 2 * gn)
    xbc = _silu(xbc)
    xs = xbc[:, 0:inner]
    bs = xbc[:, inner:inner + gn]
    cs = xbc[:, inner + gn:inner + 2 * gn]
    b_o[...] = bs.reshape(steps, nseq, gn)
    c_o[...] = cs.reshape(steps, nseq, gn)
    lane = lax.broadcasted_iota(jnp.int32, (1, LANES), 1)
    dt = jnp.where(lane < heads, _softplus(_dot(u, wdt_ref[...]) + dtb_ref[...]), 0.0)
    da = dt * (-jnp.exp(alog_ref[...]))
    sl = lambda v, t: v[t * nseq:(t + 1) * nseq, :]
    acs = []
    run = None
    for t in range(steps):
        run = sl(da, t) if run is None else run + sl(da, t)
        acs.append(run)
    last = acs[-1]
    gsum = gsum_ref[...]
    for t in range(steps):
        acc = dexp_ref[...] * sl(xs, t)
        for j in range(t + 1):
            hi, mid, lo = _split3(sl(cs, t) * sl(bs, j))
            cb = _dot(hi, gsum) + _dot(mid, gsum) + _dot(lo, gsum)
            w = cb * jnp.exp(acs[t] - acs[j]) * sl(dt, j)
            acc = acc + _expand_heads(w, e_ref) * sl(xs, j)
        yp_o[t] = acc
        ea_o[t] = _expand_heads(jnp.exp(acs[t]), e_ref)
        xw_o[t] = sl(xs, t) * _expand_heads(jnp.exp(last - acs[t]) * sl(dt, t), e_ref)
    cd_o[...] = _expand_heads(jnp.exp(last), e_ref)
    sz_o[...] = _silu(_dot(u, wmain_ref[:, 2 * width:2 * width + inner])).reshape(steps, nseq, inner)


def _sample_front_call(x3, rgc3, rgh, sc3, p, nseq_blk):
    steps, ns, d = x3.shape
    width = p["rcw"].shape[1]
    inner = p["dexp"].shape[1]
    gn = SSD_GROUPS * SSD_STATE
    conv_dim = inner + 2 * gn
    assert ns % nseq_blk == 0
    names = ["npre", "wmain", "wdt", "wgate", "rcw", "rcb", "wcat", "ba", "bx", "lam",
             "scw", "scb", "dtb", "alog", "dexp", "e", "gsum", "wprg"]
    consts = [p[n] for n in names]
    blk3 = lambda k, n: pl.BlockSpec((k, nseq_blk, n), lambda i: (0, i, 0))
    blk2 = lambda n: pl.BlockSpec((nseq_blk, n), lambda i: (i, 0))
    sds = jax.ShapeDtypeStruct
    out_shape = (sds((steps, ns, width), F32), sds((ns, width), F32), sds((CONV_W - 1, ns, width), F32),
                 sds((CONV_W - 1, ns, conv_dim), F32), sds((steps, ns, gn), F32), sds((steps, ns, gn), F32),
                 sds((steps, ns, inner), F32), sds((ns, inner), F32), sds((steps, ns, inner), F32),
                 sds((steps, ns, inner), F32), sds((steps, ns, inner), F32), sds((steps, ns, width), F32))
    out_specs = (blk3(steps, width), blk2(width), blk3(CONV_W - 1, width), blk3(CONV_W - 1, conv_dim),
                 blk3(steps, gn), blk3(steps, gn), blk3(steps, inner), blk2(inner), blk3(steps, inner),
                 blk3(steps, inner), blk3(steps, inner), blk3(steps, width))
    body = functools.partial(_sample_front_body, nseq=nseq_blk, steps=steps, width=width, inner=inner)
    return pl.pallas_call(
        body,
        grid=(ns // nseq_blk,),
        in_specs=[blk3(steps, d), blk3(CONV_W - 1, width), blk2(width), blk3(CONV_W - 1, conv_dim)]
        + [_const_spec(c.shape) for c in consts],
        out_specs=out_specs,
        out_shape=out_shape,
        compiler_params=pltpu.CompilerParams(dimension_semantics=("arbitrary",),
                                             vmem_limit_bytes=VMEM_LIMIT_BYTES),
        name="sample_front",
    )(x3, rgc3, rgh, sc3, *consts)


def _sample_state_body(s_ref, c_ref, b_ref, xw_ref, cd_ref, so_ref, yoff_ref, *, nseq, steps, inner):
    m = steps * nseq
    gw = inner // SSD_GROUPS
    cb = c_ref[...].reshape(m, c_ref.shape[-1]).astype(BF16)
    bb = b_ref[...].reshape(m, b_ref.shape[-1]).astype(BF16)
    xw = xw_ref[...].reshape(m, inner).astype(BF16)
    rowseq = lax.broadcasted_iota(jnp.int32, (m, 1), 0) % nseq
    pad = jnp.zeros((LANES - nseq, gw), F32)
    for g in range(SSD_GROUPS):
        cg = cb[:, g * SSD_STATE:(g + 1) * SSD_STATE]
        bg = bb[:, g * SSD_STATE:(g + 1) * SSD_STATE]
        xg = xw[:, g * gw:(g + 1) * gw]
        cd_t = jnp.concatenate([cd_ref[:, g * gw:(g + 1) * gw], pad], axis=0).T
        yoff = jnp.zeros((m, gw), F32)
        for q in range(nseq):
            mine = rowseq == q
            s0 = s_ref[q, pl.ds(g * gw, gw), :]
            res = lax.dot_general(cg, s0.astype(BF16), (((1,), (1,)), ((), ())), preferred_element_type=F32)
            yoff = yoff + jnp.where(mine, res, 0.0)
            xq = jnp.where(mine, xg, jnp.zeros_like(xg))
            sadd = lax.dot_general(xq, bg, (((0,), (0,)), ((), ())), preferred_element_type=F32)
            so_ref[q, pl.ds(g * gw, gw), :] = s0 * cd_t[:, q:q + 1] + sadd
        yoff_ref[:, :, pl.ds(g * gw, gw)] = yoff.reshape(steps, nseq, gw)


def _sample_state_call(state, c3, b3, xw3, cd, nseq_blk):
    ns, inner, n = state.shape
    steps = c3.shape[0]
    gn = c3.shape[-1]
    assert ns % nseq_blk == 0 and nseq_blk % SUBLANES == 0
    blk3 = lambda w: pl.BlockSpec((steps, nseq_blk, w), lambda i: (0, i, 0))
    sblk = pl.BlockSpec((nseq_blk, inner, n), lambda i: (i, 0, 0))
    body = functools.partial(_sample_state_body, nseq=nseq_blk, steps=steps, inner=inner)
    return pl.pallas_call(
        body,
        grid=(ns // nseq_blk,),
        in_specs=[sblk, blk3(gn), blk3(gn), blk3(inner), pl.BlockSpec((nseq_blk, inner), lambda i: (i, 0))],
        out_specs=(sblk, blk3(inner)),
        out_shape=(jax.ShapeDtypeStruct(state.shape, F32), jax.ShapeDtypeStruct((steps, ns, inner), F32)),
        compiler_params=pltpu.CompilerParams(dimension_semantics=("arbitrary",),
                                             vmem_limit_bytes=VMEM_LIMIT_BYTES),
        name="sample_state",
    )(state, c3, b3, xw3, cd)


def _sample_back_body(x_ref, mrg_ref, yp_ref, yoff_ref, ea_ref, sz_ref, sg_ref,
                      nw_ref, wpssd_ref, wout_ref, npost_ref, o_ref):
    y = (yp_ref[...] + yoff_ref[...] * ea_ref[...]) * sz_ref[...]
    y_ssd = _group_rmsnorm(y, nw_ref[...], SSD_GROUPS)
    m = mrg_ref[...] + sg_ref[...] * _dot(y_ssd.astype(BF16), wpssd_ref[...])
    out = _dot(m.astype(BF16), wout_ref[...])
    o_ref[...] = x_ref[...] + _rms(out, npost_ref[...])


def _sample_back_call(x2, mrg, yp, yoff, ea, sz, sg, p, tm):
    t, d = x2.shape
    assert t % tm == 0
    consts = [p[n] for n in ("nw", "wpssd", "wout", "npost")]
    rows = lambda a: pl.BlockSpec((tm, a.shape[1]), lambda i: (i, 0))
    acts = [x2, mrg, yp, yoff, ea, sz, sg]
    return pl.pallas_call(
        _sample_back_body,
        grid=(t // tm,),
        in_specs=[rows(a) for a in acts] + [_const_spec(c.shape) for c in consts],
        out_specs=pl.BlockSpec((tm, d), lambda i: (i, 0)),
        out_shape=jax.ShapeDtypeStruct((t, d), F32),
        compiler_params=pltpu.CompilerParams(dimension_semantics=("arbitrary",),
                                             vmem_limit_bytes=VMEM_LIMIT_BYTES),
        name="sample_back",
    )(*acts, *consts)


def _prep_mixer_params(n_mix_pre, n_mix_post, w_in, rg_conv_w, rg_conv_b, rg_wa, rg_ba, rg_wx, rg_bx,
                       rg_lambda, ssd_conv_w, ssd_conv_b, ssd_dt_bias, ssd_a_log, ssd_d, ssd_norm_w,
                       w_proj_rg, w_proj_ssd, w_out):
    width = rg_conv_w.shape[1]
    heads = ssd_dt_bias.shape[0]
    inner = ssd_norm_w.shape[0]
    conv_dim = ssd_conv_w.shape[1]
    main = 2 * width + inner + conv_dim
    row = lambda v: v.reshape(1, -1).astype(F32)
    padl = lambda v: jnp.pad(v.reshape(1, -1).astype(F32), ((0, 0), (0, LANES - v.shape[-1])))
    e = (jnp.arange(LANES)[:, None] == (jnp.arange(inner)[None, :] // SSD_HEAD_DIM)).astype(BF16)
    head_group = jnp.where(jnp.arange(LANES) < heads, jnp.arange(LANES) // (heads // SSD_GROUPS), -1)
    gsum = ((jnp.arange(SSD_GROUPS * SSD_STATE)[:, None] // SSD_STATE) == head_group[None, :]).astype(BF16)
    return dict(
        gsum=gsum,
        npre=row(n_mix_pre), npost=row(n_mix_post),
        wmain=w_in[:, :main].astype(BF16),
        wdt=jnp.pad(w_in[:, main:main + heads], ((0, 0), (0, LANES - heads))).astype(BF16),
        wgate=w_in[:, main + heads:].astype(BF16),
        rcw=rg_conv_w.astype(F32), rcb=row(rg_conv_b),
        wcat=jnp.concatenate([rg_wa, rg_wx], axis=-1).astype(BF16),
        ba=row(rg_ba), bx=row(rg_bx), lam=row(rg_lambda),
        scw=ssd_conv_w.astype(F32), scb=row(ssd_conv_b),
        dtb=padl(ssd_dt_bias), alog=padl(ssd_a_log),
        dexp=row(jnp.repeat(ssd_d, SSD_HEAD_DIM)), nw=row(ssd_norm_w), e=e,
        wprg=w_proj_rg.astype(BF16), wpssd=w_proj_ssd.astype(BF16), wout=w_out.astype(BF16),
    )


FFN_TILE = 512
PROMPT_TILE = 256
SAMPLE_FRONT_SEQS = 32
SAMPLE_STATE_SEQS = 8
SAMPLE_BACK_TILE = 256


def _prompt_layer(x, mp, f1, f2):
    nb, seq, d = x.shape
    x = _ffn_call(x.reshape(nb * seq, d), *f1, min(FFN_TILE, nb * seq)).reshape(nb, seq, d)
    yrg, yssd, rgh, rgc, ssd, ssdc = _prompt_mixer_call(x, mp, min(PROMPT_TILE, seq))
    flat = lambda v: v.reshape(nb * seq, v.shape[-1])
    x = _merge_call(flat(x), flat(yrg), flat(yssd), mp, min(FFN_TILE, nb * seq))
    x = _ffn_call(x, *f2, min(FFN_TILE, nb * seq)).reshape(nb, seq, d)
    heads = ssd.shape[2] // SSD_HEAD_DIM
    return x, (rgh[0], rgc[0], ssd[0].reshape(nb, heads, SSD_HEAD_DIM, SSD_STATE), ssdc[0])


def _sample_layer(x, rg_h, rg_conv, ssd, ssd_conv, mp, f1, f2):
    ns, steps, d = x.shape
    heads, hd, n = ssd.shape[1:]
    t = steps * ns
    to_steps = lambda v: jnp.transpose(v, (1, 0, 2))
    flat = lambda v: v.reshape(t, v.shape[-1])
    x = _ffn_call(to_steps(x).reshape(t, d), *f1, min(FFN_TILE, t))
    (mrg, rgh_new, rgc_new, sc_new, c3, b3, xw3, cd, yp3, ea3, sz3, sg3) = _sample_front_call(
        x.reshape(steps, ns, d), to_steps(rg_conv), rg_h, to_steps(ssd_conv), mp, min(SAMPLE_FRONT_SEQS, ns))
    ssd_new, yoff3 = _sample_state_call(ssd.reshape(ns, heads * hd, n), c3, b3, xw3, cd, SAMPLE_STATE_SEQS)
    x = _sample_back_call(x, flat(mrg), flat(yp3), flat(yoff3), flat(ea3), flat(sz3), flat(sg3), mp,
                          min(SAMPLE_BACK_TILE, t))
    x = _ffn_call(x, *f2, min(FFN_TILE, t))
    return (to_steps(x.reshape(steps, ns, d)),
            (rgh_new, to_steps(rgc_new), ssd_new.reshape(ns, heads, hd, n), to_steps(sc_new)))


def kernel(x_prompt, x_sample, state_rg_h, state_rg_conv, state_ssd, state_ssd_conv, n_ffn1_pre, n_ffn1_post, ffn1_wg, ffn1_wu, ffn1_wd, n_mix_pre, n_mix_post, w_in, rg_conv_w, rg_conv_b, rg_wa, rg_ba, rg_wx, rg_bx, rg_lambda, ssd_conv_w, ssd_conv_b, ssd_dt_bias, ssd_a_log, ssd_d, ssd_norm_w, w_proj_rg, w_proj_ssd, w_out, n_ffn2_pre, n_ffn2_post, ffn2_wg, ffn2_wu, ffn2_wd):
    depth = w_in.shape[0]
    row = lambda v: v.reshape(1, -1).astype(F32)
    yp, ys = x_prompt, x_sample
    p_new = ([], [], [], [])
    s_new = ([], [], [], [])
    for li in range(depth):
        mp = _prep_mixer_params(n_mix_pre[li], n_mix_post[li], w_in[li], rg_conv_w[li], rg_conv_b[li], rg_wa[li],
                                rg_ba[li], rg_wx[li], rg_bx[li], rg_lambda[li], ssd_conv_w[li], ssd_conv_b[li],
                                ssd_dt_bias[li], ssd_a_log[li], ssd_d[li], ssd_norm_w[li], w_proj_rg[li],
                                w_proj_ssd[li], w_out[li])
        f1 = (row(n_ffn1_pre[li]), row(n_ffn1_post[li]), ffn1_wg[li].astype(BF16), ffn1_wu[li].astype(BF16),
              ffn1_wd[li].astype(BF16))
        f2 = (row(n_ffn2_pre[li]), row(n_ffn2_post[li]), ffn2_wg[li].astype(BF16), ffn2_wu[li].astype(BF16),
              ffn2_wd[li].astype(BF16))
        yp, newp = _prompt_layer(yp, mp, f1, f2)
        ys, news = _sample_layer(ys, state_rg_h[li], state_rg_conv[li], state_ssd[li], state_ssd_conv[li], mp, f1, f2)
        for lst, v in zip(p_new, newp):
            lst.append(v)
        for lst, v in zip(s_new, news):
            lst.append(v)
    prompt_state = [jnp.stack(v, 0) for v in p_new]
    sample_state = [jnp.stack(v, 0) for v in s_new]
    return (yp, ys, *prompt_state, *sample_state)
```

```python
import functools

import jax
import jax.numpy as jnp
from jax import lax
from jax.experimental import pallas as pl
from jax.experimental.pallas import tpu as pltpu

F32 = jnp.float32
BF16 = jnp.bfloat16

EPS = 1e-6
RG_C = 8.0
CONV_W = 4
RG_BLOCKS = 8
SSD_HEAD_DIM = 64
SSD_GROUPS = 8
SSD_STATE = 128
SSD_CHUNK = 128
LANES = 128
SUBLANES = 8
VMEM_LIMIT_BYTES = 58 * 1024 * 1024


def _dot(a, b):
    return jnp.dot(a, b, preferred_element_type=F32)


def _rms(x, w):
    return x * lax.rsqrt(jnp.mean(x * x, axis=-1, keepdims=True) + EPS) * w


def _sigmoid(x):
    return 1.0 / (1.0 + jnp.exp(-x))


def _silu(x):
    return x * _sigmoid(x)


def _softplus(x):
    return jnp.maximum(x, 0.0) + jnp.log1p(jnp.exp(-jnp.abs(x)))


def _neg_expm1_2x(x, ex):
    e2 = ex * ex
    return jnp.where(jnp.abs(x) > 0.25, 1.0 - e2, -jnp.tanh(x) * (e2 + 1.0))


def _gelu_tanh(x):
    return 0.5 * x * (1.0 + jnp.tanh(0.7978845608028654 * (x + 0.044715 * (x * x * x))))


def _split(v, terms):
    out = []
    for _ in range(terms - 1):
        t = v.astype(BF16)
        out.append(t)
        v = v - t.astype(F32)
    out.append(v.astype(BF16))
    return out


def _dot_terms(v, m01, terms):
    acc = None
    for t in _split(v, terms):
        part = _dot(t, m01)
        acc = part if acc is None else acc + part
    return acc


def _expand_heads(v, e_ref, terms=3):
    return _dot_terms(v, e_ref[...], terms)


def _const_spec(shape):
    nd = len(shape)
    return pl.BlockSpec(shape, lambda *_: (0,) * nd, pipeline_mode=pl.Buffered(1))


def _param_specs(p, names):
    specs = []
    for n in names:
        if n in p["views"]:
            shape, col_block = p["views"][n]
            specs.append(pl.BlockSpec(shape, lambda *_, cb=col_block: (0, cb), pipeline_mode=pl.Buffered(1)))
        else:
            specs.append(_const_spec(p[n].shape))
    return specs


def _ffn_body(x_ref, pre_ref, post_ref, wg_ref, wu_ref, wd_ref, o_ref):
    x = x_ref[...]
    u = _rms(x, pre_ref[...]).astype(BF16)
    g = _dot(u, wg_ref[...])
    up = _dot(u, wu_ref[...])
    h = (_silu(g) * up).astype(BF16)
    y = _dot(h, wd_ref[...])
    o_ref[...] = x + 0.5 * _rms(y, post_ref[...])


def _ffn_call(x2d, pre, post, wg, wu, wd, tm):
    t, d = x2d.shape
    assert t % tm == 0
    return pl.pallas_call(
        _ffn_body,
        grid=(t // tm,),
        in_specs=[pl.BlockSpec((tm, d), lambda i: (i, 0)),
                  _const_spec(pre.shape), _const_spec(post.shape),
                  _const_spec(wg.shape), _const_spec(wu.shape), _const_spec(wd.shape)],
        out_specs=pl.BlockSpec((tm, d), lambda i: (i, 0)),
        out_shape=jax.ShapeDtypeStruct((t, d), F32),
        compiler_params=pltpu.CompilerParams(dimension_semantics=("arbitrary",),
                                             vmem_limit_bytes=VMEM_LIMIT_BYTES),
        name="ffn_half_step",
    )(x2d, pre, post, wg, wu, wd)


def _rg_gates(xc, wcat_ref, ba, bx, lam):
    blk = xc.shape[1] // RG_BLOCKS
    xb = xc.astype(BF16)
    ga, gx = [], []
    for h in range(RG_BLOCKS):
        r = _dot(xb[:, h * blk:(h + 1) * blk], wcat_ref[h])
        ga.append(r[:, :blk])
        gx.append(r[:, blk:])
    gate_a = _sigmoid(jnp.concatenate(ga, axis=1) + ba)
    gate_x = _sigmoid(jnp.concatenate(gx, axis=1) + bx)
    log_a = (-RG_C) * gate_a * _softplus(-lam)
    return log_a, gate_x


def _group_rmsnorm(y, nw, groups):
    gw = y.shape[1] // groups
    outs = []
    for g in range(groups):
        yg = y[:, g * gw:(g + 1) * gw]
        outs.append(yg * lax.rsqrt(jnp.mean(yg * yg, axis=-1, keepdims=True) + EPS))
    return jnp.concatenate(outs, axis=1) * nw


def _merge_out(x_res, y_rg, y_ssd, gates, wprg_ref, wpssd_ref, wout_ref, npost):
    width = y_rg.shape[1]
    m = (_sigmoid(gates[:, :width]) * _dot(y_rg.astype(BF16), wprg_ref[...])
         + _sigmoid(gates[:, width:]) * _dot(y_ssd.astype(BF16), wpssd_ref[...]))
    out = _dot(m.astype(BF16), wout_ref[...])
    return x_res + _rms(out, npost)


def _conv_tile(x, cbuf_ref, w_ref, b_ref):
    tm = x.shape[0]
    cbuf_ref[pl.ds(SUBLANES, tm), :] = x
    y = x * w_ref[CONV_W - 1:CONV_W, :] + b_ref[...]
    for s in range(1, CONV_W):
        y = y + cbuf_ref[pl.ds(SUBLANES - s, tm), :] * w_ref[CONV_W - 1 - s:CONV_W - s, :]
    cbuf_ref[pl.ds(0, SUBLANES), :] = cbuf_ref[pl.ds(tm, SUBLANES), :]
    return y


def _prompt_mixer_body(x_ref, npre_ref, npost_ref, wmain_ref, wdt_ref, wgate_ref,
                       rcw_ref, rcb_ref, wcat_ref, ba_ref, bx_ref, lam_ref,
                       scw_ref, scb_ref, dtb_ref, alog_ref, dexp_ref, nw_ref, e_ref,
                       wprg_ref, wpssd_ref, wout_ref,
                       o_ref, orgh_ref, orgc_ref, ossd_ref, osc_ref,
                       cb_rg, cb_xbc, a_s, u_s, h_s, hc_s, y_s, st_s,
                       *, tm, width, inner):
    b = pl.program_id(0)
    i = pl.program_id(1)
    nt = pl.num_programs(1)
    nch = tm // SSD_CHUNK
    gn = SSD_GROUPS * SSD_STATE
    heads = inner // SSD_HEAD_DIM
    gw = inner // SSD_GROUPS
    hpg = heads // SSD_GROUPS

    @pl.when(i == 0)
    def _():
        cb_rg[pl.ds(0, SUBLANES), :] = jnp.zeros((SUBLANES, width), F32)
        cb_xbc[pl.ds(0, SUBLANES), :] = jnp.zeros((SUBLANES, inner + 2 * gn), F32)
        hc_s[...] = jnp.zeros_like(hc_s)
        st_s[...] = jnp.zeros_like(st_s)

    x = x_ref[0]
    u = _rms(x, npre_ref[...]).astype(BF16)

    rg_x = _dot(u, wmain_ref[:, 0:width])
    xc = _conv_tile(rg_x, cb_rg, rcw_ref, rcb_ref)
    log_a, gate_x = _rg_gates(xc, wcat_ref, ba_ref[...], bx_ref[...], lam_ref[...])
    row = lax.broadcasted_iota(jnp.int32, (tm, 1), 0)
    is_reset = jnp.logical_and(i == 0, row == 0)
    a = jnp.exp(log_a)
    a_s[...] = jnp.where(is_reset, 0.0, a)
    u_s[...] = xc * gate_x * jnp.where(is_reset, 1.0, jnp.sqrt(_neg_expm1_2x(log_a, a)))

    def scan_step(t, h):
        h = a_s[pl.ds(t, 1), :] * h + u_s[pl.ds(t, 1), :]
        h_s[pl.ds(t, 1), :] = h
        return h

    hc_s[...] = lax.fori_loop(0, tm, scan_step, hc_s[...], unroll=8)
    rg_g = _dot(u, wmain_ref[:, width:2 * width])
    y_rg = h_s[...] * _gelu_tanh(rg_g)

    xbc = _dot(u, wmain_ref[:, 2 * width + inner:2 * width + 2 * inner + 2 * gn])
    xbc = _silu(_conv_tile(xbc, cb_xbc, scw_ref, scb_ref))
    lane = lax.broadcasted_iota(jnp.int32, (1, LANES), 1)
    dt_all = jnp.where(lane < heads, _softplus(_dot(u, wdt_ref[...]) + dtb_ref[...]), 0.0)
    da_all = dt_all * (-jnp.exp(alog_ref[...]))
    ri = lax.broadcasted_iota(jnp.int32, (SSD_CHUNK, SSD_CHUNK), 0)
    ci = lax.broadcasted_iota(jnp.int32, (SSD_CHUNK, SSD_CHUNK), 1)
    causal = ri >= ci
    tri = causal.astype(F32)
    glane = lax.broadcasted_iota(jnp.int32, (1, gw), 1) // SSD_HEAD_DIM

    for c in range(nch):
        r0 = c * SSD_CHUNK
        xs = xbc[r0:r0 + SSD_CHUNK, 0:inner]
        bs = xbc[r0:r0 + SSD_CHUNK, inner:inner + gn].astype(BF16)
        cs = xbc[r0:r0 + SSD_CHUNK, inner + gn:inner + 2 * gn].astype(BF16)
        dt = dt_all[r0:r0 + SSD_CHUNK, :]
        acs = jnp.dot(tri, da_all[r0:r0 + SSD_CHUNK, :], preferred_element_type=F32,
                      precision=lax.Precision.HIGHEST)
        acs_t = acs.T
        dt_t = dt.T
        last = acs[SSD_CHUNK - 1:SSD_CHUNK, :]
        wdec_x = _expand_heads(jnp.exp(last - acs) * dt, e_ref, terms=2)
        eacs_x = _expand_heads(jnp.exp(acs), e_ref, terms=2)
        cd_x = _expand_heads(jnp.broadcast_to(jnp.exp(last), (SUBLANES, LANES)), e_ref)[0:1, :]
        xb = xs.astype(BF16)
        xw = (xs * wdec_x).astype(BF16)
        for g in range(SSD_GROUPS):
            cg = cs[:, g * SSD_STATE:(g + 1) * SSD_STATE]
            bg = bs[:, g * SSD_STATE:(g + 1) * SSD_STATE]
            cbm = lax.dot_general(cg, bg, (((1,), (1,)), ((), ())), preferred_element_type=F32)
            xg = xb[:, g * gw:(g + 1) * gw]
            y_g = None
            for r in range(hpg):
                hh = g * hpg + r
                seg = acs[:, hh:hh + 1] - acs_t[hh:hh + 1, :]
                lm = jnp.exp(jnp.where(causal, seg, -jnp.inf))
                w = (cbm * lm * dt_t[hh:hh + 1, :]).astype(BF16)
                part = _dot(w, jnp.where(glane == r, xg, jnp.zeros_like(xg)))
                y_g = part if y_g is None else y_g + part
            st = st_s[g]
            y_g = y_g + _dot(cg, st.astype(BF16)) * eacs_x[:, g * gw:(g + 1) * gw]
            st_s[g] = st * cd_x[:, g * gw:(g + 1) * gw] + lax.dot_general(
                bg, xw[:, g * gw:(g + 1) * gw], (((0,), (0,)), ((), ())), preferred_element_type=F32)
            y_s[r0:r0 + SSD_CHUNK, g * gw:(g + 1) * gw] = y_g

    z = _dot(u, wmain_ref[:, 2 * width:2 * width + inner])
    y = (y_s[...] + dexp_ref[...] * xbc[:, 0:inner]) * _silu(z)
    y_ssd = _group_rmsnorm(y, nw_ref[...], SSD_GROUPS)

    gates = _dot(u, wgate_ref[...])
    o_ref[0] = _merge_out(x, y_rg, y_ssd, gates, wprg_ref, wpssd_ref, wout_ref, npost_ref[...])

    @pl.when(i == nt - 1)
    def _():
        orgh_ref[0, pl.ds(b, 1), :] = hc_s[...]
        orgc_ref[0, 0] = cb_rg[pl.ds(SUBLANES - (CONV_W - 1), CONV_W - 1), :]
        osc_ref[0, 0] = cb_xbc[pl.ds(SUBLANES - (CONV_W - 1), CONV_W - 1), :]
        for g in range(SSD_GROUPS):
            ossd_ref[0, 0, pl.ds(g * gw, gw), :] = st_s[g].T


def _prompt_mixer_call(x, p, tm):
    nb, seq, d = x.shape
    width = p["rcw"].shape[1]
    inner = p["dexp"].shape[1]
    gn = SSD_GROUPS * SSD_STATE
    conv_dim = inner + 2 * gn
    assert seq % tm == 0 and tm % SSD_CHUNK == 0
    names = ["npre", "npost", "wmain", "wdt", "wgate", "rcw", "rcb", "wcat", "ba", "bx", "lam",
             "scw", "scb", "dtb", "alog", "dexp", "nw", "e", "wprg", "wpssd", "wout"]
    consts = [p[n] for n in names]
    out_shapes = (
        jax.ShapeDtypeStruct((nb, seq, d), F32),
        jax.ShapeDtypeStruct((1, nb, width), F32),
        jax.ShapeDtypeStruct((1, nb, CONV_W - 1, width), F32),
        jax.ShapeDtypeStruct((1, nb, inner, SSD_STATE), F32),
        jax.ShapeDtypeStruct((1, nb, CONV_W - 1, conv_dim), F32),
    )
    out_specs = (
        pl.BlockSpec((1, tm, d), lambda b, i: (b, i, 0)),
        pl.BlockSpec((1, nb, width), lambda b, i: (0, 0, 0)),
        pl.BlockSpec((1, 1, CONV_W - 1, width), lambda b, i: (0, b, 0, 0)),
        pl.BlockSpec((1, 1, inner, SSD_STATE), lambda b, i: (0, b, 0, 0)),
        pl.BlockSpec((1, 1, CONV_W - 1, conv_dim), lambda b, i: (0, b, 0, 0)),
    )
    scratch = [
        pltpu.VMEM((tm + SUBLANES, width), F32),
        pltpu.VMEM((tm + SUBLANES, conv_dim), F32),
        pltpu.VMEM((tm, width), F32),
        pltpu.VMEM((tm, width), F32),
        pltpu.VMEM((tm, width), F32),
        pltpu.VMEM((1, width), F32),
        pltpu.VMEM((tm, inner), F32),
        pltpu.VMEM((SSD_GROUPS, SSD_STATE, inner // SSD_GROUPS), F32),
    ]
    body = functools.partial(_prompt_mixer_body, tm=tm, width=width, inner=inner)
    return pl.pallas_call(
        body,
        grid=(nb, seq // tm),
        in_specs=[pl.BlockSpec((1, tm, d), lambda b, i: (b, i, 0))] + _param_specs(p, names),
        out_specs=out_specs,
        out_shape=out_shapes,
        scratch_shapes=scratch,
        compiler_params=pltpu.CompilerParams(dimension_semantics=("arbitrary", "arbitrary"),
                                             vmem_limit_bytes=VMEM_LIMIT_BYTES),
        name="prompt_mixer",
    )(x, *consts)


def _conv_steps(x, prev, w_ref, b_ref, nseq):
    steps = x.shape[0] // nseq
    xc = jnp.concatenate([prev, x], axis=0)
    outs = []
    for t in range(steps):
        y = b_ref[...]
        for k in range(CONV_W):
            y = y + xc[(t + k) * nseq:(t + k + 1) * nseq, :] * w_ref[k:k + 1, :]
        outs.append(y)
    return jnp.concatenate(outs, axis=0), xc[steps * nseq:, :]


def _sample_front_body(x_ref, rgc_ref, rgh_ref, sc_ref,
                       npre_ref, wmain_ref, wdt_ref, wgate_ref, rcw_ref, rcb_ref, wcat_ref, ba_ref, bx_ref,
                       lam_ref, scw_ref, scb_ref, dtb_ref, alog_ref, dexp_ref, e_ref, gsum_ref, wprg_ref,
                       mrg_o, rgh_o, rgc_o, sc_o, c_o, b_o, xw_o, cd_o, yp_o, ea_o, sz_o, sg_o,
                       *, nseq, steps, width, inner):
    m = steps * nseq
    gn = SSD_GROUPS * SSD_STATE
    heads = inner // SSD_HEAD_DIM
    x = x_ref[...].reshape(m, x_ref.shape[-1])
    u = _rms(x, npre_ref[...]).astype(BF16)

    rg_x = _dot(u, wmain_ref[:, 0:width])
    xc, new_rgc = _conv_steps(rg_x, rgc_ref[...].reshape((CONV_W - 1) * nseq, width), rcw_ref, rcb_ref, nseq)
    rgc_o[...] = new_rgc.reshape(CONV_W - 1, nseq, width)
    log_a, gate_x = _rg_gates(xc, wcat_ref, ba_ref[...], bx_ref[...], lam_ref[...])
    a = jnp.exp(log_a)
    uu = xc * gate_x * jnp.sqrt(_neg_expm1_2x(log_a, a))
    h = rgh_ref[...]
    hs = []
    for t in range(steps):
        h = a[t * nseq:(t + 1) * nseq, :] * h + uu[t * nseq:(t + 1) * nseq, :]
        hs.append(h)
    rgh_o[...] = h
    y_rg = jnp.concatenate(hs, axis=0) * _gelu_tanh(_dot(u, wmain_ref[:, width:2 * width]))
    gates = _dot(u, wgate_ref[...])
    mrg_o[...] = (_sigmoid(gates[:, :width]) * _dot(y_rg.astype(BF16), wprg_ref[...])).reshape(steps, nseq, width)
    sg_o[...] = _sigmoid(gates[:, width:]).reshape(steps, nseq, width)

    xbc = _dot(u, wmain_ref[:, 2 * width + inner:2 * width + 2 * inner + 2 * gn])
    xbc, new_sc = _conv_steps(xbc, sc_ref[...].reshape((CONV_W - 1) * nseq, inner + 2 * gn), scw_ref, scb_ref, nseq)
    sc_o[...] = new_sc.reshape(CONV_W - 1, nseq, inner + 2 * gn)
    xbc = _silu(xbc)
    xs = xbc[:, 0:inner]
    bs = xbc[:, inner:inner + gn]
    cs = xbc[:, inner + gn:inner + 2 * gn]
    b_o[...] = bs.reshape(steps, nseq, gn)
    c_o[...] = cs.reshape(steps, nseq, gn)
    lane = lax.broadcasted_iota(jnp.int32, (1, LANES), 1)
    dt = jnp.where(lane < heads, _softplus(_dot(u, wdt_ref[...]) + dtb_ref[...]), 0.0)
    da = dt * (-jnp.exp(alog_ref[...]))
    sl = lambda v, t: v[t * nseq:(t + 1) * nseq, :]
    acs = []
    run = None
    for t in range(steps):
        run = sl(da, t) if run is None else run + sl(da, t)
        acs.append(run)
    last = acs[-1]
    gsum = gsum_ref[...]
    for t in range(steps):
        acc = dexp_ref[...] * sl(xs, t)
        for j in range(t + 1):
            cb = _dot_terms(sl(cs, t) * sl(bs, j), gsum, 2)
            w = cb * jnp.exp(acs[t] - acs[j]) * sl(dt, j)
            acc = acc + _expand_heads(w, e_ref, terms=2) * sl(xs, j)
        yp_o[t] = acc
        ea_o[t] = _expand_heads(jnp.exp(acs[t]), e_ref, terms=2)
        xw_o[t] = sl(xs, t) * _expand_heads(jnp.exp(last - acs[t]) * sl(dt, t), e_ref, terms=2)
    cd_o[...] = _expand_heads(jnp.exp(last), e_ref)
    sz_o[...] = _silu(_dot(u, wmain_ref[:, 2 * width:2 * width + inner])).reshape(steps, nseq, inner)


def _sample_front_call(x3, rgc3, rgh, sc3, p, nseq_blk):
    steps, ns, d = x3.shape
    width = p["rcw"].shape[1]
    inner = p["dexp"].shape[1]
    gn = SSD_GROUPS * SSD_STATE
    conv_dim = inner + 2 * gn
    assert ns % nseq_blk == 0
    names = ["npre", "wmain", "wdt", "wgate", "rcw", "rcb", "wcat", "ba", "bx", "lam",
             "scw", "scb", "dtb", "alog", "dexp", "e", "gsum", "wprg"]
    consts = [p[n] for n in names]
    blk3 = lambda k, n: pl.BlockSpec((k, nseq_blk, n), lambda i: (0, i, 0))
    blk2 = lambda n: pl.BlockSpec((nseq_blk, n), lambda i: (i, 0))
    sds = jax.ShapeDtypeStruct
    out_shape = (sds((steps, ns, width), F32), sds((ns, width), F32), sds((CONV_W - 1, ns, width), F32),
                 sds((CONV_W - 1, ns, conv_dim), F32), sds((steps, ns, gn), F32), sds((steps, ns, gn), F32),
                 sds((steps, ns, inner), F32), sds((ns, inner), F32), sds((steps, ns, inner), F32),
                 sds((steps, ns, inner), F32), sds((steps, ns, inner), F32), sds((steps, ns, width), F32))
    out_specs = (blk3(steps, width), blk2(width), blk3(CONV_W - 1, width), blk3(CONV_W - 1, conv_dim),
                 blk3(steps, gn), blk3(steps, gn), blk3(steps, inner), blk2(inner), blk3(steps, inner),
                 blk3(steps, inner), blk3(steps, inner), blk3(steps, width))
    body = functools.partial(_sample_front_body, nseq=nseq_blk, steps=steps, width=width, inner=inner)
    return pl.pallas_call(
        body,
        grid=(ns // nseq_blk,),
        in_specs=[blk3(steps, d), blk3(CONV_W - 1, width), blk2(width), blk3(CONV_W - 1, conv_dim)]
        + _param_specs(p, names),
        out_specs=out_specs,
        out_shape=out_shape,
        compiler_params=pltpu.CompilerParams(dimension_semantics=("arbitrary",),
                                             vmem_limit_bytes=VMEM_LIMIT_BYTES),
        name="sample_front",
    )(x3, rgc3, rgh, sc3, *consts)


def _sample_state_body(s_ref, c_ref, b_ref, xw_ref, cd_ref, so_ref, yoff_ref, *, nseq, steps, inner):
    m = steps * nseq
    gw = inner // SSD_GROUPS
    cb = c_ref[...].reshape(m, c_ref.shape[-1]).astype(BF16)
    bb = b_ref[...].reshape(m, b_ref.shape[-1]).astype(BF16)
    xw = xw_ref[...].reshape(m, inner).astype(BF16)
    rowseq = lax.broadcasted_iota(jnp.int32, (m, 1), 0) % nseq
    pad = jnp.zeros((LANES - nseq, gw), F32)
    for g in range(SSD_GROUPS):
        cg = cb[:, g * SSD_STATE:(g + 1) * SSD_STATE]
        bg = bb[:, g * SSD_STATE:(g + 1) * SSD_STATE]
        xg = xw[:, g * gw:(g + 1) * gw]
        cd_t = jnp.concatenate([cd_ref[:, g * gw:(g + 1) * gw], pad], axis=0).T
        yoff = jnp.zeros((m, gw), F32)
        for q in range(nseq):
            mine = rowseq == q
            s0 = s_ref[q, pl.ds(g * gw, gw), :]
            res = lax.dot_general(cg, s0.astype(BF16), (((1,), (1,)), ((), ())), preferred_element_type=F32)
            yoff = yoff + jnp.where(mine, res, 0.0)
            xq = jnp.where(mine, xg, jnp.zeros_like(xg))
            sadd = lax.dot_general(xq, bg, (((0,), (0,)), ((), ())), preferred_element_type=F32)
            so_ref[q, pl.ds(g * gw, gw), :] = s0 * cd_t[:, q:q + 1] + sadd
        yoff_ref[:, :, pl.ds(g * gw, gw)] = yoff.reshape(steps, nseq, gw)


def _sample_state_call(state, c3, b3, xw3, cd, nseq_blk):
    ns, inner, n = state.shape
    steps = c3.shape[0]
    gn = c3.shape[-1]
    assert ns % nseq_blk == 0 and nseq_blk % SUBLANES == 0
    blk3 = lambda w: pl.BlockSpec((steps, nseq_blk, w), lambda i: (0, i, 0))
    sblk = pl.BlockSpec((nseq_blk, inner, n), lambda i: (i, 0, 0))
    body = functools.partial(_sample_state_body, nseq=nseq_blk, steps=steps, inner=inner)
    return pl.pallas_call(
        body,
        grid=(ns // nseq_blk,),
        in_specs=[sblk, blk3(gn), blk3(gn), blk3(inner), pl.BlockSpec((nseq_blk, inner), lambda i: (i, 0))],
        out_specs=(sblk, blk3(inner)),
        out_shape=(jax.ShapeDtypeStruct(state.shape, F32), jax.ShapeDtypeStruct((steps, ns, inner), F32)),
        compiler_params=pltpu.CompilerParams(dimension_semantics=("arbitrary",),
                                             vmem_limit_bytes=VMEM_LIMIT_BYTES),
        name="sample_state",
    )(state, c3, b3, xw3, cd)


def _sample_back_body(x_ref, mrg_ref, yp_ref, yoff_ref, ea_ref, sz_ref, sg_ref,
                      nw_ref, wpssd_ref, wout_ref, npost_ref, o_ref):
    y = (yp_ref[...] + yoff_ref[...] * ea_ref[...]) * sz_ref[...]
    y_ssd = _group_rmsnorm(y, nw_ref[...], SSD_GROUPS)
    m = mrg_ref[...] + sg_ref[...] * _dot(y_ssd.astype(BF16), wpssd_ref[...])
    out = _dot(m.astype(BF16), wout_ref[...])
    o_ref[...] = x_ref[...] + _rms(out, npost_ref[...])


def _sample_back_call(x2, mrg, yp, yoff, ea, sz, sg, p, tm):
    t, d = x2.shape
    assert t % tm == 0
    consts = [p[n] for n in ("nw", "wpssd", "wout", "npost")]
    rows = lambda a: pl.BlockSpec((tm, a.shape[1]), lambda i: (i, 0))
    acts = [x2, mrg, yp, yoff, ea, sz, sg]
    return pl.pallas_call(
        _sample_back_body,
        grid=(t // tm,),
        in_specs=[rows(a) for a in acts] + [_const_spec(c.shape) for c in consts],
        out_specs=pl.BlockSpec((tm, d), lambda i: (i, 0)),
        out_shape=jax.ShapeDtypeStruct((t, d), F32),
        compiler_params=pltpu.CompilerParams(dimension_semantics=("arbitrary",),
                                             vmem_limit_bytes=VMEM_LIMIT_BYTES),
        name="sample_back",
    )(*acts, *consts)


def _prep_mixer_params(n_mix_pre, n_mix_post, w_in, rg_conv_w, rg_conv_b, rg_wa, rg_ba, rg_wx, rg_bx,
                       rg_lambda, ssd_conv_w, ssd_conv_b, ssd_dt_bias, ssd_a_log, ssd_d, ssd_norm_w,
                       w_proj_rg, w_proj_ssd, w_out):
    width = rg_conv_w.shape[1]
    heads = ssd_dt_bias.shape[0]
    inner = ssd_norm_w.shape[0]
    conv_dim = ssd_conv_w.shape[1]
    main = 2 * width + inner + conv_dim
    row = lambda v: v.reshape(1, -1).astype(F32)
    padl = lambda v: jnp.pad(v.reshape(1, -1).astype(F32), ((0, 0), (0, LANES - v.shape[-1])))
    e = (jnp.arange(LANES)[:, None] == (jnp.arange(inner)[None, :] // SSD_HEAD_DIM)).astype(BF16)
    head_group = jnp.where(jnp.arange(LANES) < heads, jnp.arange(LANES) // (heads // SSD_GROUPS), -1)
    gsum = ((jnp.arange(SSD_GROUPS * SSD_STATE)[:, None] // SSD_STATE) == head_group[None, :]).astype(BF16)
    d_in = w_in.shape[0]
    wpack = jnp.concatenate([w_in[:, :main], w_in[:, main + heads:], w_in[:, main:main + heads],
                             jnp.zeros((d_in, LANES - heads), w_in.dtype)], axis=1).astype(BF16)
    assert main % (2 * width) == 0 and (main + 2 * width) % LANES == 0
    views = dict(wmain=((d_in, main), 0), wgate=((d_in, 2 * width), main // (2 * width)),
                 wdt=((d_in, LANES), (main + 2 * width) // LANES))
    return dict(
        gsum=gsum, views=views,
        npre=row(n_mix_pre), npost=row(n_mix_post),
        wmain=wpack, wdt=wpack, wgate=wpack,
        rcw=rg_conv_w.astype(F32), rcb=row(rg_conv_b),
        wcat=jnp.concatenate([rg_wa, rg_wx], axis=-1).astype(BF16),
        ba=row(rg_ba), bx=row(rg_bx), lam=row(rg_lambda),
        scw=ssd_conv_w.astype(F32), scb=row(ssd_conv_b),
        dtb=padl(ssd_dt_bias), alog=padl(ssd_a_log),
        dexp=row(jnp.repeat(ssd_d, SSD_HEAD_DIM)), nw=row(ssd_norm_w), e=e,
        wprg=w_proj_rg.astype(BF16), wpssd=w_proj_ssd.astype(BF16), wout=w_out.astype(BF16),
    )


FFN_TILE = 512
PROMPT_TILE = 256
SAMPLE_FRONT_SEQS = 32
SAMPLE_STATE_SEQS = 8
SAMPLE_BACK_TILE = 256


def _prompt_layer(x, mp, f1, f2):
    nb, seq, d = x.shape
    x = _ffn_call(x.reshape(nb * seq, d), *f1, min(FFN_TILE, nb * seq)).reshape(nb, seq, d)
    x, rgh, rgc, ssd, ssdc = _prompt_mixer_call(x, mp, min(PROMPT_TILE, seq))
    x = _ffn_call(x.reshape(nb * seq, d), *f2, min(FFN_TILE, nb * seq)).reshape(nb, seq, d)
    heads = ssd.shape[2] // SSD_HEAD_DIM
    return x, (rgh[0], rgc[0], ssd[0].reshape(nb, heads, SSD_HEAD_DIM, SSD_STATE), ssdc[0])


def _sample_layer(x, rg_h, rg_conv, ssd, ssd_conv, mp, f1, f2):
    ns, steps, d = x.shape
    heads, hd, n = ssd.shape[1:]
    t = steps * ns
    to_steps = lambda v: jnp.transpose(v, (1, 0, 2))
    flat = lambda v: v.reshape(t, v.shape[-1])
    x = _ffn_call(to_steps(x).reshape(t, d), *f1, min(FFN_TILE, t))
    (mrg, rgh_new, rgc_new, sc_new, c3, b3, xw3, cd, yp3, ea3, sz3, sg3) = _sample_front_call(
        x.reshape(steps, ns, d), to_steps(rg_conv), rg_h, to_steps(ssd_conv), mp, min(SAMPLE_FRONT_SEQS, ns))
    ssd_new, yoff3 = _sample_state_call(ssd.reshape(ns, heads * hd, n), c3, b3, xw3, cd, SAMPLE_STATE_SEQS)
    x = _sample_back_call(x, flat(mrg), flat(yp3), flat(yoff3), flat(ea3), flat(sz3), flat(sg3), mp,
                          min(SAMPLE_BACK_TILE, t))
    x = _ffn_call(x, *f2, min(FFN_TILE, t))
    return (to_steps(x.reshape(steps, ns, d)),
            (rgh_new, to_steps(rgc_new), ssd_new.reshape(ns, heads, hd, n), to_steps(sc_new)))


def kernel(x_prompt, x_sample, state_rg_h, state_rg_conv, state_ssd, state_ssd_conv, n_ffn1_pre, n_ffn1_post, ffn1_wg, ffn1_wu, ffn1_wd, n_mix_pre, n_mix_post, w_in, rg_conv_w, rg_conv_b, rg_wa, rg_ba, rg_wx, rg_bx, rg_lambda, ssd_conv_w, ssd_conv_b, ssd_dt_bias, ssd_a_log, ssd_d, ssd_norm_w, w_proj_rg, w_proj_ssd, w_out, n_ffn2_pre, n_ffn2_post, ffn2_wg, ffn2_wu, ffn2_wd):
    depth = w_in.shape[0]
    row = lambda v: v.reshape(1, -1).astype(F32)
    yp, ys = x_prompt, x_sample
    p_new = ([], [], [], [])
    s_new = ([], [], [], [])
    for li in range(depth):
        mp = _prep_mixer_params(n_mix_pre[li], n_mix_post[li], w_in[li], rg_conv_w[li], rg_conv_b[li], rg_wa[li],
                                rg_ba[li], rg_wx[li], rg_bx[li], rg_lambda[li], ssd_conv_w[li], ssd_conv_b[li],
                                ssd_dt_bias[li], ssd_a_log[li], ssd_d[li], ssd_norm_w[li], w_proj_rg[li],
                                w_proj_ssd[li], w_out[li])
        f1 = (row(n_ffn1_pre[li]), row(n_ffn1_post[li]), ffn1_wg[li].astype(BF16), ffn1_wu[li].astype(BF16),
              ffn1_wd[li].astype(BF16))
        f2 = (row(n_ffn2_pre[li]), row(n_ffn2_post[li]), ffn2_wg[li].astype(BF16), ffn2_wu[li].astype(BF16),
              ffn2_wd[li].astype(BF16))
        yp, newp = _prompt_layer(yp, mp, f1, f2)
        ys, news = _sample_layer(ys, state_rg_h[li], state_rg_conv[li], state_ssd[li], state_ssd_conv[li], mp, f1, f2)
        for lst, v in zip(p_new, newp):
            lst.append(v)
        for lst, v in zip(s_new, news):
            lst.append(v)
    prompt_state = [jnp.stack(v, 0) for v in p_new]
    sample_state = [jnp.stack(v, 0) for v in s_new]
    return (yp, ys, *prompt_state, *sample_state)
```

```python
import functools

import jax
import jax.numpy as jnp
from jax import lax
from jax.experimental import pallas as pl
from jax.experimental.pallas import tpu as pltpu

F32 = jnp.float32
BF16 = jnp.bfloat16

EPS = 1e-6
RG_C = 8.0
CONV_W = 4
RG_BLOCKS = 8
SSD_HEAD_DIM = 64
SSD_GROUPS = 8
SSD_STATE = 128
SSD_CHUNK = 128
LANES = 128
SUBLANES = 8
VMEM_LIMIT_BYTES = 58 * 1024 * 1024


def _dot(a, b):
    return jnp.dot(a, b, preferred_element_type=F32)


def _rms(x, w):
    return x * lax.rsqrt(jnp.mean(x * x, axis=-1, keepdims=True) + EPS) * w


def _sigmoid(x):
    return 1.0 / (1.0 + jnp.exp(-x))


def _silu(x):
    return x * _sigmoid(x)


def _softplus(x):
    return jnp.maximum(x, 0.0) + jnp.log1p(jnp.exp(-jnp.abs(x)))


def _neg_expm1_2x(x, ex):
    e2 = ex * ex
    return jnp.where(jnp.abs(x) > 0.25, 1.0 - e2, -jnp.tanh(x) * (e2 + 1.0))


def _gelu_tanh(x):
    return 0.5 * x * (1.0 + jnp.tanh(0.7978845608028654 * (x + 0.044715 * (x * x * x))))


def _split(v, terms):
    out = []
    for _ in range(terms - 1):
        t = v.astype(BF16)
        out.append(t)
        v = v - t.astype(F32)
    out.append(v.astype(BF16))
    return out


def _dot_terms(v, m01, terms):
    acc = None
    for t in _split(v, terms):
        part = _dot(t, m01)
        acc = part if acc is None else acc + part
    return acc


def _expand_heads(v, e_ref, terms=3):
    return _dot_terms(v, e_ref[...], terms)


def _const_spec(shape):
    nd = len(shape)
    return pl.BlockSpec(shape, lambda *_: (0,) * nd, pipeline_mode=pl.Buffered(1))


def _param_specs(p, names):
    specs = []
    for n in names:
        if n in p["views"]:
            shape, col_block = p["views"][n]
            specs.append(pl.BlockSpec(shape, lambda *_, cb=col_block: (0, cb), pipeline_mode=pl.Buffered(1)))
        else:
            specs.append(_const_spec(p[n].shape))
    return specs


FFN_ROW_SPLIT = 2


def _ffn_body(x_ref, pre_ref, post_ref, wg_ref, wu_ref, wd_ref, o_ref):
    half = x_ref.shape[0] // FFN_ROW_SPLIT
    for r in range(FFN_ROW_SPLIT):
        rows = slice(r * half, (r + 1) * half)
        x = x_ref[rows, :]
        u = _rms(x, pre_ref[...]).astype(BF16)
        g = _dot(u, wg_ref[...])
        up = _dot(u, wu_ref[...])
        h = (_silu(g) * up).astype(BF16)
        y = _dot(h, wd_ref[...])
        o_ref[rows, :] = x + 0.5 * _rms(y, post_ref[...])


def _ffn_call(x2d, pre, post, wg, wu, wd, tm):
    t, d = x2d.shape
    assert t % tm == 0
    return pl.pallas_call(
        _ffn_body,
        grid=(t // tm,),
        in_specs=[pl.BlockSpec((tm, d), lambda i: (i, 0)),
                  _const_spec(pre.shape), _const_spec(post.shape),
                  _const_spec(wg.shape), _const_spec(wu.shape), _const_spec(wd.shape)],
        out_specs=pl.BlockSpec((tm, d), lambda i: (i, 0)),
        out_shape=jax.ShapeDtypeStruct((t, d), F32),
        compiler_params=pltpu.CompilerParams(dimension_semantics=("arbitrary",),
                                             vmem_limit_bytes=VMEM_LIMIT_BYTES),
        name="ffn_half_step",
    )(x2d, pre, post, wg, wu, wd)


def _rg_gates(xc, wcat_ref, ba, bx, lam):
    blk = xc.shape[1] // RG_BLOCKS
    xb = xc.astype(BF16)
    ga, gx = [], []
    for h in range(RG_BLOCKS):
        r = _dot(xb[:, h * blk:(h + 1) * blk], wcat_ref[h])
        ga.append(r[:, :blk])
        gx.append(r[:, blk:])
    gate_a = _sigmoid(jnp.concatenate(ga, axis=1) + ba)
    gate_x = _sigmoid(jnp.concatenate(gx, axis=1) + bx)
    log_a = (-RG_C) * gate_a * _softplus(-lam)
    return log_a, gate_x


def _group_rmsnorm(y, nw, groups):
    gw = y.shape[1] // groups
    outs = []
    for g in range(groups):
        yg = y[:, g * gw:(g + 1) * gw]
        outs.append(yg * lax.rsqrt(jnp.mean(yg * yg, axis=-1, keepdims=True) + EPS))
    return jnp.concatenate(outs, axis=1) * nw


def _merge_out(x_res, y_rg, y_ssd, gates, wprg_ref, wpssd_ref, wout_ref, npost):
    width = y_rg.shape[1]
    m = (_sigmoid(gates[:, :width]) * _dot(y_rg.astype(BF16), wprg_ref[...])
         + _sigmoid(gates[:, width:]) * _dot(y_ssd.astype(BF16), wpssd_ref[...]))
    out = _dot(m.astype(BF16), wout_ref[...])
    return x_res + _rms(out, npost)


def _conv_tile(x, cbuf_ref, w_ref, b_ref):
    tm = x.shape[0]
    cbuf_ref[pl.ds(SUBLANES, tm), :] = x
    y = x * w_ref[CONV_W - 1:CONV_W, :] + b_ref[...]
    for s in range(1, CONV_W):
        y = y + cbuf_ref[pl.ds(SUBLANES - s, tm), :] * w_ref[CONV_W - 1 - s:CONV_W - s, :]
    cbuf_ref[pl.ds(0, SUBLANES), :] = cbuf_ref[pl.ds(tm, SUBLANES), :]
    return y


def _prompt_mixer_body(x_ref, npre_ref, npost_ref, wmain_ref, wdt_ref, wgate_ref,
                       rcw_ref, rcb_ref, wcat_ref, ba_ref, bx_ref, lam_ref,
                       scw_ref, scb_ref, dtb_ref, alog_ref, dexp_ref, nw_ref, e_ref,
                       wprg_ref, wpssd_ref, wout_ref,
                       o_ref, orgh_ref, orgc_ref, ossd_ref, osc_ref,
                       cb_rg, cb_xbc, a_s, u_s, h_s, hc_s, y_s, st_s,
                       *, tm, width, inner):
    b = pl.program_id(0)
    i = pl.program_id(1)
    nt = pl.num_programs(1)
    nch = tm // SSD_CHUNK
    gn = SSD_GROUPS * SSD_STATE
    heads = inner // SSD_HEAD_DIM
    gw = inner // SSD_GROUPS
    hpg = heads // SSD_GROUPS

    @pl.when(i == 0)
    def _():
        cb_rg[pl.ds(0, SUBLANES), :] = jnp.zeros((SUBLANES, width), F32)
        cb_xbc[pl.ds(0, SUBLANES), :] = jnp.zeros((SUBLANES, inner + 2 * gn), F32)
        hc_s[...] = jnp.zeros_like(hc_s)
        st_s[...] = jnp.zeros_like(st_s)

    x = x_ref[0]
    u = _rms(x, npre_ref[...]).astype(BF16)

    rg_x = _dot(u, wmain_ref[:, 0:width])
    xc = _conv_tile(rg_x, cb_rg, rcw_ref, rcb_ref)
    log_a, gate_x = _rg_gates(xc, wcat_ref, ba_ref[...], bx_ref[...], lam_ref[...])
    row = lax.broadcasted_iota(jnp.int32, (tm, 1), 0)
    is_reset = jnp.logical_and(i == 0, row == 0)
    a = jnp.exp(log_a)
    a_s[...] = jnp.where(is_reset, 0.0, a)
    u_s[...] = xc * gate_x * jnp.where(is_reset, 1.0, jnp.sqrt(_neg_expm1_2x(log_a, a)))

    def scan_step(t, h):
        h = a_s[pl.ds(t, 1), :] * h + u_s[pl.ds(t, 1), :]
        h_s[pl.ds(t, 1), :] = h
        return h

    hc_s[...] = lax.fori_loop(0, tm, scan_step, hc_s[...], unroll=8)
    rg_g = _dot(u, wmain_ref[:, width:2 * width])
    y_rg = h_s[...] * _gelu_tanh(rg_g)

    xbc = _dot(u, wmain_ref[:, 2 * width + inner:2 * width + 2 * inner + 2 * gn])
    xbc = _silu(_conv_tile(xbc, cb_xbc, scw_ref, scb_ref))
    lane = lax.broadcasted_iota(jnp.int32, (1, LANES), 1)
    dt_all = jnp.where(lane < heads, _softplus(_dot(u, wdt_ref[...]) + dtb_ref[...]), 0.0)
    da_all = dt_all * (-jnp.exp(alog_ref[...]))
    ri = lax.broadcasted_iota(jnp.int32, (SSD_CHUNK, SSD_CHUNK), 0)
    ci = lax.broadcasted_iota(jnp.int32, (SSD_CHUNK, SSD_CHUNK), 1)
    causal = ri >= ci
    tri = causal.astype(F32)
    glane = lax.broadcasted_iota(jnp.int32, (1, gw), 1) // SSD_HEAD_DIM

    for c in range(nch):
        r0 = c * SSD_CHUNK
        xs = xbc[r0:r0 + SSD_CHUNK, 0:inner]
        bs = xbc[r0:r0 + SSD_CHUNK, inner:inner + gn].astype(BF16)
        cs = xbc[r0:r0 + SSD_CHUNK, inner + gn:inner + 2 * gn].astype(BF16)
        dt = dt_all[r0:r0 + SSD_CHUNK, :]
        acs = jnp.dot(tri, da_all[r0:r0 + SSD_CHUNK, :], preferred_element_type=F32,
                      precision=lax.Precision.HIGHEST)
        acs_t = acs.T
        dt_t = dt.T
        last = acs[SSD_CHUNK - 1:SSD_CHUNK, :]
        wdec_x = _expand_heads(jnp.exp(last - acs) * dt, e_ref, terms=2)
        eacs_x = _expand_heads(jnp.exp(acs), e_ref, terms=2)
        cd_x = _expand_heads(jnp.broadcast_to(jnp.exp(last), (SUBLANES, LANES)), e_ref)[0:1, :]
        xb = xs.astype(BF16)
        xw = (xs * wdec_x).astype(BF16)
        for g in range(SSD_GROUPS):
            cg = cs[:, g * SSD_STATE:(g + 1) * SSD_STATE]
            bg = bs[:, g * SSD_STATE:(g + 1) * SSD_STATE]
            cbm = lax.dot_general(cg, bg, (((1,), (1,)), ((), ())), preferred_element_type=F32)
            xg = xb[:, g * gw:(g + 1) * gw]
            y_g = None
            for r in range(hpg):
                hh = g * hpg + r
                seg = acs[:, hh:hh + 1] - acs_t[hh:hh + 1, :]
                lm = jnp.exp(jnp.where(causal, seg, -jnp.inf))
                w = (cbm * lm * dt_t[hh:hh + 1, :]).astype(BF16)
                part = _dot(w, jnp.where(glane == r, xg, jnp.zeros_like(xg)))
                y_g = part if y_g is None else y_g + part
            st = st_s[g]
            y_g = y_g + _dot(cg, st.astype(BF16)) * eacs_x[:, g * gw:(g + 1) * gw]
            st_s[g] = st * cd_x[:, g * gw:(g + 1) * gw] + lax.dot_general(
                bg, xw[:, g * gw:(g + 1) * gw], (((0,), (0,)), ((), ())), preferred_element_type=F32)
            y_s[r0:r0 + SSD_CHUNK, g * gw:(g + 1) * gw] = y_g

    z = _dot(u, wmain_ref[:, 2 * width:2 * width + inner])
    y = (y_s[...] + dexp_ref[...] * xbc[:, 0:inner]) * _silu(z)
    y_ssd = _group_rmsnorm(y, nw_ref[...], SSD_GROUPS)

    gates = _dot(u, wgate_ref[...])
    o_ref[0] = _merge_out(x, y_rg, y_ssd, gates, wprg_ref, wpssd_ref, wout_ref, npost_ref[...])

    @pl.when(i == nt - 1)
    def _():
        orgh_ref[0, pl.ds(b, 1), :] = hc_s[...]
        orgc_ref[0, 0] = cb_rg[pl.ds(SUBLANES - (CONV_W - 1), CONV_W - 1), :]
        osc_ref[0, 0] = cb_xbc[pl.ds(SUBLANES - (CONV_W - 1), CONV_W - 1), :]
        for g in range(SSD_GROUPS):
            ossd_ref[0, 0, pl.ds(g * gw, gw), :] = st_s[g].T


def _prompt_mixer_call(x, p, tm):
    nb, seq, d = x.shape
    width = p["rcw"].shape[1]
    inner = p["dexp"].shape[1]
    gn = SSD_GROUPS * SSD_STATE
    conv_dim = inner + 2 * gn
    assert seq % tm == 0 and tm % SSD_CHUNK == 0
    names = ["npre", "npost", "wmain", "wdt", "wgate", "rcw", "rcb", "wcat", "ba", "bx", "lam",
             "scw", "scb", "dtb", "alog", "dexp", "nw", "e", "wprg", "wpssd", "wout"]
    consts = [p[n] for n in names]
    out_shapes = (
        jax.ShapeDtypeStruct((nb, seq, d), F32),
        jax.ShapeDtypeStruct((1, nb, width), F32),
        jax.ShapeDtypeStruct((1, nb, CONV_W - 1, width), F32),
        jax.ShapeDtypeStruct((1, nb, inner, SSD_STATE), F32),
        jax.ShapeDtypeStruct((1, nb, CONV_W - 1, conv_dim), F32),
    )
    out_specs = (
        pl.BlockSpec((1, tm, d), lambda b, i: (b, i, 0)),
        pl.BlockSpec((1, nb, width), lambda b, i: (0, 0, 0)),
        pl.BlockSpec((1, 1, CONV_W - 1, width), lambda b, i: (0, b, 0, 0)),
        pl.BlockSpec((1, 1, inner, SSD_STATE), lambda b, i: (0, b, 0, 0)),
        pl.BlockSpec((1, 1, CONV_W - 1, conv_dim), lambda b, i: (0, b, 0, 0)),
    )
    scratch = [
        pltpu.VMEM((tm + SUBLANES, width), F32),
        pltpu.VMEM((tm + SUBLANES, conv_dim), F32),
        pltpu.VMEM((tm, width), F32),
        pltpu.VMEM((tm, width), F32),
        pltpu.VMEM((tm, width), F32),
        pltpu.VMEM((1, width), F32),
        pltpu.VMEM((tm, inner), F32),
        pltpu.VMEM((SSD_GROUPS, SSD_STATE, inner // SSD_GROUPS), F32),
    ]
    body = functools.partial(_prompt_mixer_body, tm=tm, width=width, inner=inner)
    return pl.pallas_call(
        body,
        grid=(nb, seq // tm),
        in_specs=[pl.BlockSpec((1, tm, d), lambda b, i: (b, i, 0))] + _param_specs(p, names),
        out_specs=out_specs,
        out_shape=out_shapes,
        scratch_shapes=scratch,
        compiler_params=pltpu.CompilerParams(dimension_semantics=("arbitrary", "arbitrary"),
                                             vmem_limit_bytes=VMEM_LIMIT_BYTES),
        name="prompt_mixer",
    )(x, *consts)


def _conv_steps(x, prev, w_ref, b_ref, nseq):
    steps = x.shape[0] // nseq
    xc = jnp.concatenate([prev, x], axis=0)
    outs = []
    for t in range(steps):
        y = b_ref[...]
        for k in range(CONV_W):
            y = y + xc[(t + k) * nseq:(t + k + 1) * nseq, :] * w_ref[k:k + 1, :]
        outs.append(y)
    return jnp.concatenate(outs, axis=0), xc[steps * nseq:, :]


def _sample_front_body(x_ref, rgc_ref, rgh_ref, sc_ref,
                       npre_ref, wmain_ref, wdt_ref, wgate_ref, rcw_ref, rcb_ref, wcat_ref, ba_ref, bx_ref,
                       lam_ref, scw_ref, scb_ref, dtb_ref, alog_ref, dexp_ref, e_ref, gsum_ref, wprg_ref,
                       mrg_o, rgh_o, rgc_o, sc_o, c_o, b_o, xw_o, cd_o, yp_o, ea_o, sz_o, sg_o,
                       *, nseq, steps, width, inner):
    m = steps * nseq
    gn = SSD_GROUPS * SSD_STATE
    heads = inner // SSD_HEAD_DIM
    x = x_ref[...].reshape(m, x_ref.shape[-1])
    u = _rms(x, npre_ref[...]).astype(BF16)

    rg_x = _dot(u, wmain_ref[:, 0:width])
    xc, new_rgc = _conv_steps(rg_x, rgc_ref[...].reshape((CONV_W - 1) * nseq, width), rcw_ref, rcb_ref, nseq)
    rgc_o[...] = new_rgc.reshape(CONV_W - 1, nseq, width)
    log_a, gate_x = _rg_gates(xc, wcat_ref, ba_ref[...], bx_ref[...], lam_ref[...])
    a = jnp.exp(log_a)
    uu = xc * gate_x * jnp.sqrt(_neg_expm1_2x(log_a, a))
    h = rgh_ref[...]
    hs = []
    for t in range(steps):
        h = a[t * nseq:(t + 1) * nseq, :] * h + uu[t * nseq:(t + 1) * nseq, :]
        hs.append(h)
    rgh_o[...] = h
    y_rg = jnp.concatenate(hs, axis=0) * _gelu_tanh(_dot(u, wmain_ref[:, width:2 * width]))
    gates = _dot(u, wgate_ref[...])
    mrg_o[...] = (_sigmoid(gates[:, :width]) * _dot(y_rg.astype(BF16), wprg_ref[...])).reshape(steps, nseq, width)
    sg_o[...] = _sigmoid(gates[:, width:]).reshape(steps, nseq, width)

    xbc = _dot(u, wmain_ref[:, 2 * width + inner:2 * width + 2 * inner + 2 * gn])
    xbc, new_sc = _conv_steps(xbc, sc_ref[...].reshape((CONV_W - 1) * nseq, inner + 2 * gn), scw_ref, scb_ref, nseq)
    sc_o[...] = new_sc.reshape(CONV_W - 1, nseq, inner + 2 * gn)
    xbc = _silu(xbc)
    xs = xbc[:, 0:inner]
    bs = xbc[:, inner:inner + gn]
    cs = xbc[:, inner + gn:inner + 2 * gn]
    b_o[...] = bs.reshape(steps, nseq, gn)
    c_o[...] = cs.reshape(steps, nseq, gn)
    lane = lax.broadcasted_iota(jnp.int32, (1, LANES), 1)
    dt = jnp.where(lane < heads, _softplus(_dot(u, wdt_ref[...]) + dtb_ref[...]), 0.0)
    da = dt * (-jnp.exp(alog_ref[...]))
    sl = lambda v, t: v[t * nseq:(t + 1) * nseq, :]
    acs = []
    run = None
    for t in range(steps):
        run = sl(da, t) if run is None else run + sl(da, t)
        acs.append(run)
    last = acs[-1]
    gsum = gsum_ref[...]
    for t in range(steps):
        acc = dexp_ref[...] * sl(xs, t)
        for j in range(t + 1):
            cb = _dot_terms(sl(cs, t) * sl(bs, j), gsum, 2)
            w = cb * jnp.exp(acs[t] - acs[j]) * sl(dt, j)
            acc = acc + _expand_heads(w, e_ref, terms=2) * sl(xs, j)
        yp_o[t] = acc
        ea_o[t] = _expand_heads(jnp.exp(acs[t]), e_ref, terms=2)
        xw_o[t] = sl(xs, t) * _expand_heads(jnp.exp(last - acs[t]) * sl(dt, t), e_ref, terms=2)
    cd_o[...] = _expand_heads(jnp.exp(last), e_ref)
    sz_o[...] = _silu(_dot(u, wmain_ref[:, 2 * width:2 * width + inner])).reshape(steps, nseq, inner)


def _sample_front_call(x3, rgc3, rgh, sc3, p, nseq_blk):
    steps, ns, d = x3.shape
    width = p["rcw"].shape[1]
    inner = p["dexp"].shape[1]
    gn = SSD_GROUPS * SSD_STATE
    conv_dim = inner + 2 * gn
    assert ns % nseq_blk == 0
    names = ["npre", "wmain", "wdt", "wgate", "rcw", "rcb", "wcat", "ba", "bx", "lam",
             "scw", "scb", "dtb", "alog", "dexp", "e", "gsum", "wprg"]
    consts = [p[n] for n in names]
    blk3 = lambda k, n: pl.BlockSpec((k, nseq_blk, n), lambda i: (0, i, 0))
    blk2 = lambda n: pl.BlockSpec((nseq_blk, n), lambda i: (i, 0))
    sds = jax.ShapeDtypeStruct
    out_shape = (sds((steps, ns, width), F32), sds((ns, width), F32), sds((CONV_W - 1, ns, width), F32),
                 sds((CONV_W - 1, ns, conv_dim), F32), sds((steps, ns, gn), F32), sds((steps, ns, gn), F32),
                 sds((steps, ns, inner), F32), sds((ns, inner), F32), sds((steps, ns, inner), F32),
                 sds((steps, ns, inner), F32), sds((steps, ns, inner), F32), sds((steps, ns, width), F32))
    out_specs = (blk3(steps, width), blk2(width), blk3(CONV_W - 1, width), blk3(CONV_W - 1, conv_dim),
                 blk3(steps, gn), blk3(steps, gn), blk3(steps, inner), blk2(inner), blk3(steps, inner),
                 blk3(steps, inner), blk3(steps, inner), blk3(steps, width))
    body = functools.partial(_sample_front_body, nseq=nseq_blk, steps=steps, width=width, inner=inner)
    return pl.pallas_call(
        body,
        grid=(ns // nseq_blk,),
        in_specs=[blk3(steps, d), blk3(CONV_W - 1, width), blk2(width), blk3(CONV_W - 1, conv_dim)]
        + _param_specs(p, names),
        out_specs=out_specs,
        out_shape=out_shape,
        compiler_params=pltpu.CompilerParams(dimension_semantics=("arbitrary",),
                                             vmem_limit_bytes=VMEM_LIMIT_BYTES),
        name="sample_front",
    )(x3, rgc3, rgh, sc3, *consts)


def _sample_state_body(s_ref, c_ref, b_ref, xw_ref, cd_ref, so_ref, yoff_ref, *, nseq, steps, inner):
    m = steps * nseq
    gw = inner // SSD_GROUPS
    cb = c_ref[...].reshape(m, c_ref.shape[-1]).astype(BF16)
    bb = b_ref[...].reshape(m, b_ref.shape[-1]).astype(BF16)
    xw = xw_ref[...].reshape(m, inner).astype(BF16)
    rowseq = lax.broadcasted_iota(jnp.int32, (m, 1), 0) % nseq
    pad = jnp.zeros((LANES - nseq, gw), F32)
    for g in range(SSD_GROUPS):
        cg = cb[:, g * SSD_STATE:(g + 1) * SSD_STATE]
        bg = bb[:, g * SSD_STATE:(g + 1) * SSD_STATE]
        xg = xw[:, g * gw:(g + 1) * gw]
        cd_t = jnp.concatenate([cd_ref[:, g * gw:(g + 1) * gw], pad], axis=0).T
        yoff = jnp.zeros((m, gw), F32)
        for q in range(nseq):
            mine = rowseq == q
            s0 = s_ref[q, pl.ds(g * gw, gw), :]
            res = lax.dot_general(cg, s0.astype(BF16), (((1,), (1,)), ((), ())), preferred_element_type=F32)
            yoff = yoff + jnp.where(mine, res, 0.0)
            xq = jnp.where(mine, xg, jnp.zeros_like(xg))
            sadd = lax.dot_general(xq, bg, (((0,), (0,)), ((), ())), preferred_element_type=F32)
            so_ref[q, pl.ds(g * gw, gw), :] = s0 * cd_t[:, q:q + 1] + sadd
        yoff_ref[:, :, pl.ds(g * gw, gw)] = yoff.reshape(steps, nseq, gw)


def _sample_state_call(state, c3, b3, xw3, cd, nseq_blk):
    ns, inner, n = state.shape
    steps = c3.shape[0]
    gn = c3.shape[-1]
    assert ns % nseq_blk == 0 and nseq_blk % SUBLANES == 0
    blk3 = lambda w: pl.BlockSpec((steps, nseq_blk, w), lambda i: (0, i, 0))
    sblk = pl.BlockSpec((nseq_blk, inner, n), lambda i: (i, 0, 0))
    body = functools.partial(_sample_state_body, nseq=nseq_blk, steps=steps, inner=inner)
    return pl.pallas_call(
        body,
        grid=(ns // nseq_blk,),
        in_specs=[sblk, blk3(gn), blk3(gn), blk3(inner), pl.BlockSpec((nseq_blk, inner), lambda i: (i, 0))],
        out_specs=(sblk, blk3(inner)),
        out_shape=(jax.ShapeDtypeStruct(state.shape, F32), jax.ShapeDtypeStruct((steps, ns, inner), F32)),
        compiler_params=pltpu.CompilerParams(dimension_semantics=("arbitrary",),
                                             vmem_limit_bytes=VMEM_LIMIT_BYTES),
        name="sample_state",
    )(state, c3, b3, xw3, cd)


def _sample_back_body(x_ref, mrg_ref, yp_ref, yoff_ref, ea_ref, sz_ref, sg_ref,
                      nw_ref, wpssd_ref, wout_ref, npost_ref, o_ref):
    y = (yp_ref[...] + yoff_ref[...] * ea_ref[...]) * sz_ref[...]
    y_ssd = _group_rmsnorm(y, nw_ref[...], SSD_GROUPS)
    m = mrg_ref[...] + sg_ref[...] * _dot(y_ssd.astype(BF16), wpssd_ref[...])
    out = _dot(m.astype(BF16), wout_ref[...])
    o_ref[...] = x_ref[...] + _rms(out, npost_ref[...])


def _sample_back_call(x2, mrg, yp, yoff, ea, sz, sg, p, tm):
    t, d = x2.shape
    assert t % tm == 0
    consts = [p[n] for n in ("nw", "wpssd", "wout", "npost")]
    rows = lambda a: pl.BlockSpec((tm, a.shape[1]), lambda i: (i, 0))
    acts = [x2, mrg, yp, yoff, ea, sz, sg]
    return pl.pallas_call(
        _sample_back_body,
        grid=(t // tm,),
        in_specs=[rows(a) for a in acts] + [_const_spec(c.shape) for c in consts],
        out_specs=pl.BlockSpec((tm, d), lambda i: (i, 0)),
        out_shape=jax.ShapeDtypeStruct((t, d), F32),
        compiler_params=pltpu.CompilerParams(dimension_semantics=("arbitrary",),
                                             vmem_limit_bytes=VMEM_LIMIT_BYTES),
        name="sample_back",
    )(*acts, *consts)


def _prep_mixer_params(n_mix_pre, n_mix_post, w_in, rg_conv_w, rg_conv_b, rg_wa, rg_ba, rg_wx, rg_bx,
                       rg_lambda, ssd_conv_w, ssd_conv_b, ssd_dt_bias, ssd_a_log, ssd_d, ssd_norm_w,
                       w_proj_rg, w_proj_ssd, w_out):
    width = rg_conv_w.shape[1]
    heads = ssd_dt_bias.shape[0]
    inner = ssd_norm_w.shape[0]
    conv_dim = ssd_conv_w.shape[1]
    main = 2 * width + inner + conv_dim
    row = lambda v: v.reshape(1, -1).astype(F32)
    padl = lambda v: jnp.pad(v.reshape(1, -1).astype(F32), ((0, 0), (0, LANES - v.shape[-1])))
    e = (jnp.arange(LANES)[:, None] == (jnp.arange(inner)[None, :] // SSD_HEAD_DIM)).astype(BF16)
    head_group = jnp.where(jnp.arange(LANES) < heads, jnp.arange(LANES) // (heads // SSD_GROUPS), -1)
    gsum = ((jnp.arange(SSD_GROUPS * SSD_STATE)[:, None] // SSD_STATE) == head_group[None, :]).astype(BF16)
    d_in = w_in.shape[0]
    w_bf = w_in.astype(BF16)
    assert main % LANES == 0 and main + LANES <= w_in.shape[1]
    views = dict(wmain=((d_in, main), 0), wdt=((d_in, LANES), main // LANES))
    return dict(
        gsum=gsum, views=views,
        npre=row(n_mix_pre), npost=row(n_mix_post),
        wmain=w_bf, wdt=w_bf, wgate=w_bf[:, main + heads:],
        rcw=rg_conv_w.astype(F32), rcb=row(rg_conv_b),
        wcat=jnp.concatenate([rg_wa, rg_wx], axis=-1).astype(BF16),
        ba=row(rg_ba), bx=row(rg_bx), lam=row(rg_lambda),
        scw=ssd_conv_w.astype(F32), scb=row(ssd_conv_b),
        dtb=padl(ssd_dt_bias), alog=padl(ssd_a_log),
        dexp=row(jnp.repeat(ssd_d, SSD_HEAD_DIM)), nw=row(ssd_norm_w), e=e,
        wprg=w_proj_rg.astype(BF16), wpssd=w_proj_ssd.astype(BF16), wout=w_out.astype(BF16),
    )


FFN_TILE = 512
PROMPT_TILE = 256
SAMPLE_FRONT_SEQS = 32
SAMPLE_STATE_SEQS = 8
SAMPLE_BACK_TILE = 256


def _prompt_layer(x, mp, f1, f2):
    nb, seq, d = x.shape
    x = _ffn_call(x.reshape(nb * seq, d), *f1, min(FFN_TILE, nb * seq)).reshape(nb, seq, d)
    x, rgh, rgc, ssd, ssdc = _prompt_mixer_call(x, mp, min(PROMPT_TILE, seq))
    x = _ffn_call(x.reshape(nb * seq, d), *f2, min(FFN_TILE, nb * seq)).reshape(nb, seq, d)
    heads = ssd.shape[2] // SSD_HEAD_DIM
    return x, (rgh[0], rgc[0], ssd[0].reshape(nb, heads, SSD_HEAD_DIM, SSD_STATE), ssdc[0])


def _sample_layer(x, rg_h, rg_conv, ssd, ssd_conv, mp, f1, f2):
    ns, steps, d = x.shape
    heads, hd, n = ssd.shape[1:]
    t = steps * ns
    to_steps = lambda v: jnp.transpose(v, (1, 0, 2))
    flat = lambda v: v.reshape(t, v.shape[-1])
    x = _ffn_call(to_steps(x).reshape(t, d), *f1, min(FFN_TILE, t))
    (mrg, rgh_new, rgc_new, sc_new, c3, b3, xw3, cd, yp3, ea3, sz3, sg3) = _sample_front_call(
        x.reshape(steps, ns, d), to_steps(rg_conv), rg_h, to_steps(ssd_conv), mp, min(SAMPLE_FRONT_SEQS, ns))
    ssd_new, yoff3 = _sample_state_call(ssd.reshape(ns, heads * hd, n), c3, b3, xw3, cd, SAMPLE_STATE_SEQS)
    x = _sample_back_call(x, flat(mrg), flat(yp3), flat(yoff3), flat(ea3), flat(sz3), flat(sg3), mp,
                          min(SAMPLE_BACK_TILE, t))
    x = _ffn_call(x, *f2, min(FFN_TILE, t))
    return (to_steps(x.reshape(steps, ns, d)),
            (rgh_new, to_steps(rgc_new), ssd_new.reshape(ns, heads, hd, n), to_steps(sc_new)))


def kernel(x_prompt, x_sample, state_rg_h, state_rg_conv, state_ssd, state_ssd_conv, n_ffn1_pre, n_ffn1_post, ffn1_wg, ffn1_wu, ffn1_wd, n_mix_pre, n_mix_post, w_in, rg_conv_w, rg_conv_b, rg_wa, rg_ba, rg_wx, rg_bx, rg_lambda, ssd_conv_w, ssd_conv_b, ssd_dt_bias, ssd_a_log, ssd_d, ssd_norm_w, w_proj_rg, w_proj_ssd, w_out, n_ffn2_pre, n_ffn2_post, ffn2_wg, ffn2_wu, ffn2_wd):
    depth = w_in.shape[0]
    row = lambda v: v.reshape(1, -1).astype(F32)
    yp, ys = x_prompt, x_sample
    p_new = ([], [], [], [])
    s_new = ([], [], [], [])
    for li in range(depth):
        mp = _prep_mixer_params(n_mix_pre[li], n_mix_post[li], w_in[li], rg_conv_w[li], rg_conv_b[li], rg_wa[li],
                                rg_ba[li], rg_wx[li], rg_bx[li], rg_lambda[li], ssd_conv_w[li], ssd_conv_b[li],
                                ssd_dt_bias[li], ssd_a_log[li], ssd_d[li], ssd_norm_w[li], w_proj_rg[li],
                                w_proj_ssd[li], w_out[li])
        f1 = (row(n_ffn1_pre[li]), row(n_ffn1_post[li]), ffn1_wg[li].astype(BF16), ffn1_wu[li].astype(BF16),
              ffn1_wd[li].astype(BF16))
        f2 = (row(n_ffn2_pre[li]), row(n_ffn2_post[li]), ffn2_wg[li].astype(BF16), ffn2_wu[li].astype(BF16),
              ffn2_wd[li].astype(BF16))
        yp, newp = _prompt_layer(yp, mp, f1, f2)
        ys, news = _sample_layer(ys, state_rg_h[li], state_rg_conv[li], state_ssd[li], state_ssd_conv[li], mp, f1, f2)
        for lst, v in zip(p_new, newp):
            lst.append(v)
        for lst, v in zip(s_new, news):
            lst.append(v)
    prompt_state = [jnp.stack(v, 0) for v in p_new]
    sample_state = [jnp.stack(v, 0) for v in s_new]
    return (yp, ys, *prompt_state, *sample_state)
```

```python
import functools

import jax
import jax.numpy as jnp
from jax import lax
from jax.experimental import pallas as pl
from jax.experimental.pallas import tpu as pltpu

F32 = jnp.float32
BF16 = jnp.bfloat16

EPS = 1e-6
RG_C = 8.0
CONV_W = 4
RG_BLOCKS = 8
SSD_HEAD_DIM = 64
SSD_GROUPS = 8
SSD_STATE = 128
SSD_CHUNK = 128
LANES = 128
SUBLANES = 8
VMEM_LIMIT_BYTES = 62 * 1024 * 1024


def _dot(a, b):
    return jnp.dot(a, b, preferred_element_type=F32)


def _rms(x, w):
    return x * lax.rsqrt(jnp.mean(x * x, axis=-1, keepdims=True) + EPS) * w


def _sigmoid(x):
    return 1.0 / (1.0 + jnp.exp(-x))


def _silu(x):
    return x * _sigmoid(x)


def _softplus(x):
    return jnp.maximum(x, 0.0) + jnp.log1p(jnp.exp(-jnp.abs(x)))


def _neg_expm1_2x(x, ex):
    e2 = ex * ex
    return jnp.where(jnp.abs(x) > 0.25, 1.0 - e2, -jnp.tanh(x) * (e2 + 1.0))


def _gelu_tanh(x):
    return 0.5 * x * (1.0 + jnp.tanh(0.7978845608028654 * (x + 0.044715 * (x * x * x))))


def _split(v, terms):
    out = []
    for _ in range(terms - 1):
        t = v.astype(BF16)
        out.append(t)
        v = v - t.astype(F32)
    out.append(v.astype(BF16))
    return out


def _dot_terms(v, m01, terms):
    acc = None
    for t in _split(v, terms):
        part = _dot(t, m01)
        acc = part if acc is None else acc + part
    return acc


def _expand_heads(v, e_ref, terms=3):
    return _dot_terms(v, e_ref[...], terms)


def _const_spec(shape):
    nd = len(shape)
    return pl.BlockSpec(shape, lambda *_: (0,) * nd, pipeline_mode=pl.Buffered(1))


def _param_specs(p, names):
    specs = []
    for n in names:
        if n in p["views"]:
            shape, col_block = p["views"][n]
            specs.append(pl.BlockSpec(shape, lambda *_, cb=col_block: (0, cb), pipeline_mode=pl.Buffered(1)))
        else:
            specs.append(_const_spec(p[n].shape))
    return specs


FFN_ROW_SPLIT = 2


def _ffn_body(x_ref, pre_ref, post_ref, wg_ref, wu_ref, wd_ref, o_ref):
    half = x_ref.shape[0] // FFN_ROW_SPLIT
    for r in range(FFN_ROW_SPLIT):
        rows = slice(r * half, (r + 1) * half)
        x = x_ref[rows, :]
        u = _rms(x, pre_ref[...]).astype(BF16)
        g = _dot(u, wg_ref[...])
        up = _dot(u, wu_ref[...])
        h = (_silu(g) * up).astype(BF16)
        y = _dot(h, wd_ref[...])
        o_ref[rows, :] = x + 0.5 * _rms(y, post_ref[...])


def _ffn_call(x2d, pre, post, wg, wu, wd, tm):
    t, d = x2d.shape
    assert t % tm == 0
    return pl.pallas_call(
        _ffn_body,
        grid=(t // tm,),
        in_specs=[pl.BlockSpec((tm, d), lambda i: (i, 0)),
                  _const_spec(pre.shape), _const_spec(post.shape),
                  _const_spec(wg.shape), _const_spec(wu.shape), _const_spec(wd.shape)],
        out_specs=pl.BlockSpec((tm, d), lambda i: (i, 0)),
        out_shape=jax.ShapeDtypeStruct((t, d), F32),
        compiler_params=pltpu.CompilerParams(dimension_semantics=("arbitrary",),
                                             vmem_limit_bytes=VMEM_LIMIT_BYTES),
        name="ffn_half_step",
    )(x2d, pre, post, wg, wu, wd)


def _rg_gates(xc, wcat_ref, ba, bx, lam):
    blk = xc.shape[1] // RG_BLOCKS
    xb = xc.astype(BF16)
    ga, gx = [], []
    for h in range(RG_BLOCKS):
        r = _dot(xb[:, h * blk:(h + 1) * blk], wcat_ref[h])
        ga.append(r[:, :blk])
        gx.append(r[:, blk:])
    gate_a = _sigmoid(jnp.concatenate(ga, axis=1) + ba)
    gate_x = _sigmoid(jnp.concatenate(gx, axis=1) + bx)
    log_a = (-RG_C) * gate_a * _softplus(-lam)
    return log_a, gate_x


def _group_rmsnorm(y, nw, groups):
    gw = y.shape[1] // groups
    outs = []
    for g in range(groups):
        yg = y[:, g * gw:(g + 1) * gw]
        outs.append(yg * lax.rsqrt(jnp.mean(yg * yg, axis=-1, keepdims=True) + EPS))
    return jnp.concatenate(outs, axis=1) * nw


def _merge_out(x_res, y_rg, y_ssd, gates, wprg_ref, wpssd_ref, wout_ref, npost):
    width = y_rg.shape[1]
    m = (_sigmoid(gates[:, :width]) * _dot(y_rg.astype(BF16), wprg_ref[...])
         + _sigmoid(gates[:, width:]) * _dot(y_ssd.astype(BF16), wpssd_ref[...]))
    out = _dot(m.astype(BF16), wout_ref[...])
    return x_res + _rms(out, npost)


def _conv_tile(x, cbuf_ref, w_ref, b_ref):
    tm = x.shape[0]
    cbuf_ref[pl.ds(SUBLANES, tm), :] = x
    y = x * w_ref[CONV_W - 1:CONV_W, :] + b_ref[...]
    for s in range(1, CONV_W):
        y = y + cbuf_ref[pl.ds(SUBLANES - s, tm), :] * w_ref[CONV_W - 1 - s:CONV_W - s, :]
    cbuf_ref[pl.ds(0, SUBLANES), :] = cbuf_ref[pl.ds(tm, SUBLANES), :]
    return y


def _prompt_mixer_body(x_ref, npre_ref, npost_ref, wmain_ref, wdt_ref, wgate_ref,
                       rcw_ref, rcb_ref, wcat_ref, ba_ref, bx_ref, lam_ref,
                       scw_ref, scb_ref, dtb_ref, alog_ref, dexp_ref, nw_ref, e_ref,
                       wprg_ref, wpssd_ref, wout_ref,
                       o_ref, orgh_ref, orgc_ref, ossd_ref, osc_ref,
                       cb_rg, cb_xbc, hc_s, y_s, st_s,
                       *, tm, width, inner):
    b = pl.program_id(0)
    i = pl.program_id(1)
    nt = pl.num_programs(1)
    nch = tm // SSD_CHUNK
    gn = SSD_GROUPS * SSD_STATE
    heads = inner // SSD_HEAD_DIM
    gw = inner // SSD_GROUPS
    hpg = heads // SSD_GROUPS

    @pl.when(i == 0)
    def _():
        cb_rg[pl.ds(0, SUBLANES), :] = jnp.zeros((SUBLANES, width), F32)
        cb_xbc[pl.ds(0, SUBLANES), :] = jnp.zeros((SUBLANES, inner + 2 * gn), F32)
        hc_s[...] = jnp.zeros_like(hc_s)
        st_s[...] = jnp.zeros_like(st_s)

    x = x_ref[0]
    u = _rms(x, npre_ref[...]).astype(BF16)

    rg_x = _dot(u, wmain_ref[:, 0:width])
    xc = _conv_tile(rg_x, cb_rg, rcw_ref, rcb_ref)
    log_a, gate_x = _rg_gates(xc, wcat_ref, ba_ref[...], bx_ref[...], lam_ref[...])
    row = lax.broadcasted_iota(jnp.int32, (tm, 1), 0)
    is_reset = jnp.logical_and(i == 0, row == 0)
    a = jnp.exp(log_a)
    uu = xc * gate_x * jnp.where(is_reset, 1.0, jnp.sqrt(_neg_expm1_2x(log_a, a)))
    a = jnp.where(is_reset, 0.0, a)
    ngrp = tm // SUBLANES
    a3 = a.reshape(ngrp, SUBLANES, width)
    u3 = uu.reshape(ngrp, SUBLANES, width)
    sub = lax.broadcasted_iota(jnp.int32, (1, SUBLANES, 1), 1)
    for s in (1, 2, 4):
        keep = sub >= s
        u3 = jnp.where(keep, a3 * pltpu.roll(u3, s, 1) + u3, u3)
        a3 = jnp.where(keep, a3 * pltpu.roll(a3, s, 1), a3)
    h_prev = hc_s[...]
    hs = []
    for g in range(ngrp):
        hg = a3[g] * h_prev + u3[g]
        hs.append(hg)
        h_prev = hg[SUBLANES - 1:SUBLANES, :]
    hc_s[...] = h_prev
    rg_g = _dot(u, wmain_ref[:, width:2 * width])
    y_rg = jnp.concatenate(hs, axis=0) * _gelu_tanh(rg_g)

    xbc = _dot(u, wmain_ref[:, 2 * width + inner:2 * width + 2 * inner + 2 * gn])
    xbc = _silu(_conv_tile(xbc, cb_xbc, scw_ref, scb_ref))
    lane = lax.broadcasted_iota(jnp.int32, (1, LANES), 1)
    dt_all = jnp.where(lane < heads, _softplus(_dot(u, wdt_ref[...]) + dtb_ref[...]), 0.0)
    da_all = dt_all * (-jnp.exp(alog_ref[...]))
    ri = lax.broadcasted_iota(jnp.int32, (SSD_CHUNK, SSD_CHUNK), 0)
    ci = lax.broadcasted_iota(jnp.int32, (SSD_CHUNK, SSD_CHUNK), 1)
    causal = ri >= ci
    tri = causal.astype(F32)
    glane = lax.broadcasted_iota(jnp.int32, (1, gw), 1) // SSD_HEAD_DIM

    for c in range(nch):
        r0 = c * SSD_CHUNK
        xs = xbc[r0:r0 + SSD_CHUNK, 0:inner]
        bs = xbc[r0:r0 + SSD_CHUNK, inner:inner + gn].astype(BF16)
        cs = xbc[r0:r0 + SSD_CHUNK, inner + gn:inner + 2 * gn].astype(BF16)
        dt = dt_all[r0:r0 + SSD_CHUNK, :]
        acs = jnp.dot(tri, da_all[r0:r0 + SSD_CHUNK, :], preferred_element_type=F32,
                      precision=lax.Precision.HIGHEST)
        acs_t = acs.T
        dt_t = dt.T
        last = acs[SSD_CHUNK - 1:SSD_CHUNK, :]
        wdec_x = _expand_heads(jnp.exp(last - acs) * dt, e_ref, terms=2)
        eacs_x = _expand_heads(jnp.exp(acs), e_ref, terms=2)
        cd_x = _expand_heads(jnp.broadcast_to(jnp.exp(last), (SUBLANES, LANES)), e_ref)[0:1, :]
        xb = xs.astype(BF16)
        xw = (xs * wdec_x).astype(BF16)
        for g in range(SSD_GROUPS):
            cg = cs[:, g * SSD_STATE:(g + 1) * SSD_STATE]
            bg = bs[:, g * SSD_STATE:(g + 1) * SSD_STATE]
            cbm = lax.dot_general(cg, bg, (((1,), (1,)), ((), ())), preferred_element_type=F32)
            xg = xb[:, g * gw:(g + 1) * gw]
            y_g = None
            for r in range(hpg):
                hh = g * hpg + r
                seg = acs[:, hh:hh + 1] - acs_t[hh:hh + 1, :]
                lm = jnp.exp(jnp.where(causal, seg, -jnp.inf))
                w = (cbm * lm * dt_t[hh:hh + 1, :]).astype(BF16)
                part = _dot(w, jnp.where(glane == r, xg, jnp.zeros_like(xg)))
                y_g = part if y_g is None else y_g + part
            st = st_s[g]
            y_g = y_g + _dot(cg, st.astype(BF16)) * eacs_x[:, g * gw:(g + 1) * gw]
            st_s[g] = st * cd_x[:, g * gw:(g + 1) * gw] + lax.dot_general(
                bg, xw[:, g * gw:(g + 1) * gw], (((0,), (0,)), ((), ())), preferred_element_type=F32)
            y_s[r0:r0 + SSD_CHUNK, g * gw:(g + 1) * gw] = y_g

    z = _dot(u, wmain_ref[:, 2 * width:2 * width + inner])
    y = (y_s[...] + dexp_ref[...] * xbc[:, 0:inner]) * _silu(z)
    y_ssd = _group_rmsnorm(y, nw_ref[...], SSD_GROUPS)

    gates = _dot(u, wgate_ref[...])
    o_ref[0] = _merge_out(x, y_rg, y_ssd, gates, wprg_ref, wpssd_ref, wout_ref, npost_ref[...])

    @pl.when(i == nt - 1)
    def _():
        orgh_ref[0, pl.ds(b, 1), :] = hc_s[...]
        orgc_ref[0, 0] = cb_rg[pl.ds(SUBLANES - (CONV_W - 1), CONV_W - 1), :]
        osc_ref[0, 0] = cb_xbc[pl.ds(SUBLANES - (CONV_W - 1), CONV_W - 1), :]
        for g in range(SSD_GROUPS):
            ossd_ref[0, 0, pl.ds(g * gw, gw), :] = st_s[g].T


def _prompt_mixer_call(x, p, tm):
    nb, seq, d = x.shape
    width = p["rcw"].shape[1]
    inner = p["dexp"].shape[1]
    gn = SSD_GROUPS * SSD_STATE
    conv_dim = inner + 2 * gn
    assert seq % tm == 0 and tm % SSD_CHUNK == 0
    names = ["npre", "npost", "wmain", "wdt", "wgate", "rcw", "rcb", "wcat", "ba", "bx", "lam",
             "scw", "scb", "dtb", "alog", "dexp", "nw", "e", "wprg", "wpssd", "wout"]
    consts = [p[n] for n in names]
    out_shapes = (
        jax.ShapeDtypeStruct((nb, seq, d), F32),
        jax.ShapeDtypeStruct((1, nb, width), F32),
        jax.ShapeDtypeStruct((1, nb, CONV_W - 1, width), F32),
        jax.ShapeDtypeStruct((1, nb, inner, SSD_STATE), F32),
        jax.ShapeDtypeStruct((1, nb, CONV_W - 1, conv_dim), F32),
    )
    out_specs = (
        pl.BlockSpec((1, tm, d), lambda b, i: (b, i, 0)),
        pl.BlockSpec((1, nb, width), lambda b, i: (0, 0, 0)),
        pl.BlockSpec((1, 1, CONV_W - 1, width), lambda b, i: (0, b, 0, 0)),
        pl.BlockSpec((1, 1, inner, SSD_STATE), lambda b, i: (0, b, 0, 0)),
        pl.BlockSpec((1, 1, CONV_W - 1, conv_dim), lambda b, i: (0, b, 0, 0)),
    )
    scratch = [
        pltpu.VMEM((tm + SUBLANES, width), F32),
        pltpu.VMEM((tm + SUBLANES, conv_dim), F32),
        pltpu.VMEM((1, width), F32),
        pltpu.VMEM((tm, inner), F32),
        pltpu.VMEM((SSD_GROUPS, SSD_STATE, inner // SSD_GROUPS), F32),
    ]
    body = functools.partial(_prompt_mixer_body, tm=tm, width=width, inner=inner)
    return pl.pallas_call(
        body,
        grid=(nb, seq // tm),
        in_specs=[pl.BlockSpec((1, tm, d), lambda b, i: (b, i, 0))] + _param_specs(p, names),
        out_specs=out_specs,
        out_shape=out_shapes,
        scratch_shapes=scratch,
        compiler_params=pltpu.CompilerParams(dimension_semantics=("arbitrary", "arbitrary"),
                                             vmem_limit_bytes=VMEM_LIMIT_BYTES),
        name="prompt_mixer",
    )(x, *consts)


def _conv_steps(x, prev, w_ref, b_ref, nseq):
    steps = x.shape[0] // nseq
    xc = jnp.concatenate([prev, x], axis=0)
    outs = []
    for t in range(steps):
        y = b_ref[...]
        for k in range(CONV_W):
            y = y + xc[(t + k) * nseq:(t + k + 1) * nseq, :] * w_ref[k:k + 1, :]
        outs.append(y)
    return jnp.concatenate(outs, axis=0), xc[steps * nseq:, :]


def _sample_front_body(x_ref, rgc_ref, rgh_ref, sc_ref,
                       npre_ref, wmain_ref, wdt_ref, wgate_ref, rcw_ref, rcb_ref, wcat_ref, ba_ref, bx_ref,
                       lam_ref, scw_ref, scb_ref, dtb_ref, alog_ref, dexp_ref, e_ref, gsum_ref, wprg_ref,
                       mrg_o, rgh_o, rgc_o, sc_o, c_o, b_o, xw_o, cd_o, yp_o, ea_o, sz_o, sg_o,
                       *, nseq, steps, width, inner):
    m = steps * nseq
    gn = SSD_GROUPS * SSD_STATE
    heads = inner // SSD_HEAD_DIM
    x = x_ref[...].reshape(m, x_ref.shape[-1])
    u = _rms(x, npre_ref[...]).astype(BF16)

    rg_x = _dot(u, wmain_ref[:, 0:width])
    xc, new_rgc = _conv_steps(rg_x, rgc_ref[...].reshape((CONV_W - 1) * nseq, width), rcw_ref, rcb_ref, nseq)
    rgc_o[...] = new_rgc.reshape(CONV_W - 1, nseq, width)
    log_a, gate_x = _rg_gates(xc, wcat_ref, ba_ref[...], bx_ref[...], lam_ref[...])
    a = jnp.exp(log_a)
    uu = xc * gate_x * jnp.sqrt(_neg_expm1_2x(log_a, a))
    h = rgh_ref[...]
    hs = []
    for t in range(steps):
        h = a[t * nseq:(t + 1) * nseq, :] * h + uu[t * nseq:(t + 1) * nseq, :]
        hs.append(h)
    rgh_o[...] = h
    y_rg = jnp.concatenate(hs, axis=0) * _gelu_tanh(_dot(u, wmain_ref[:, width:2 * width]))
    gates = _dot(u, wgate_ref[...])
    mrg_o[...] = (_sigmoid(gates[:, :width]) * _dot(y_rg.astype(BF16), wprg_ref[...])).reshape(steps, nseq, width)
    sg_o[...] = _sigmoid(gates[:, width:]).reshape(steps, nseq, width)

    xbc = _dot(u, wmain_ref[:, 2 * width + inner:2 * width + 2 * inner + 2 * gn])
    xbc, new_sc = _conv_steps(xbc, sc_ref[...].reshape((CONV_W - 1) * nseq, inner + 2 * gn), scw_ref, scb_ref, nseq)
    sc_o[...] = new_sc.reshape(CONV_W - 1, nseq, inner + 2 * gn)
    xbc = _silu(xbc)
    xs = xbc[:, 0:inner]
    bs = xbc[:, inner:inner + gn]
    cs = xbc[:, inner + gn:inner + 2 * gn]
    b_o[...] = bs.reshape(steps, nseq, gn)
    c_o[...] = cs.reshape(steps, nseq, gn)
    lane = lax.broadcasted_iota(jnp.int32, (1, LANES), 1)
    dt = jnp.where(lane < heads, _softplus(_dot(u, wdt_ref[...]) + dtb_ref[...]), 0.0)
    da = dt * (-jnp.exp(alog_ref[...]))
    sl = lambda v, t: v[t * nseq:(t + 1) * nseq, :]
    acs = []
    run = None
    for t in range(steps):
        run = sl(da, t) if run is None else run + sl(da, t)
        acs.append(run)
    last = acs[-1]
    gsum = gsum_ref[...]
    for t in range(steps):
        acc = dexp_ref[...] * sl(xs, t)
        for j in range(t + 1):
            cb = _dot_terms(sl(cs, t) * sl(bs, j), gsum, 2)
            w = cb * jnp.exp(acs[t] - acs[j]) * sl(dt, j)
            acc = acc + _expand_heads(w, e_ref, terms=2) * sl(xs, j)
        yp_o[t] = acc
        ea_o[t] = _expand_heads(jnp.exp(acs[t]), e_ref, terms=2)
        xw_o[t] = sl(xs, t) * _expand_heads(jnp.exp(last - acs[t]) * sl(dt, t), e_ref, terms=2)
    cd_o[...] = _expand_heads(jnp.exp(last), e_ref)
    sz_o[...] = _silu(_dot(u, wmain_ref[:, 2 * width:2 * width + inner])).reshape(steps, nseq, inner)


def _sample_front_call(x3, rgc3, rgh, sc3, p, nseq_blk):
    steps, ns, d = x3.shape
    width = p["rcw"].shape[1]
    inner = p["dexp"].shape[1]
    gn = SSD_GROUPS * SSD_STATE
    conv_dim = inner + 2 * gn
    assert ns % nseq_blk == 0
    names = ["npre", "wmain", "wdt", "wgate", "rcw", "rcb", "wcat", "ba", "bx", "lam",
             "scw", "scb", "dtb", "alog", "dexp", "e", "gsum", "wprg"]
    consts = [p[n] for n in names]
    blk3 = lambda k, n: pl.BlockSpec((k, nseq_blk, n), lambda i: (0, i, 0))
    blk2 = lambda n: pl.BlockSpec((nseq_blk, n), lambda i: (i, 0))
    sds = jax.ShapeDtypeStruct
    out_shape = (sds((steps, ns, width), F32), sds((ns, width), F32), sds((CONV_W - 1, ns, width), F32),
                 sds((CONV_W - 1, ns, conv_dim), F32), sds((steps, ns, gn), F32), sds((steps, ns, gn), F32),
                 sds((steps, ns, inner), F32), sds((ns, inner), F32), sds((steps, ns, inner), F32),
                 sds((steps, ns, inner), F32), sds((steps, ns, inner), F32), sds((steps, ns, width), F32))
    out_specs = (blk3(steps, width), blk2(width), blk3(CONV_W - 1, width), blk3(CONV_W - 1, conv_dim),
                 blk3(steps, gn), blk3(steps, gn), blk3(steps, inner), blk2(inner), blk3(steps, inner),
                 blk3(steps, inner), blk3(steps, inner), blk3(steps, width))
    body = functools.partial(_sample_front_body, nseq=nseq_blk, steps=steps, width=width, inner=inner)
    return pl.pallas_call(
        body,
        grid=(ns // nseq_blk,),
        in_specs=[blk3(steps, d), blk3(CONV_W - 1, width), blk2(width), blk3(CONV_W - 1, conv_dim)]
        + _param_specs(p, names),
        out_specs=out_specs,
        out_shape=out_shape,
        compiler_params=pltpu.CompilerParams(dimension_semantics=("arbitrary",),
                                             vmem_limit_bytes=VMEM_LIMIT_BYTES),
        name="sample_front",
    )(x3, rgc3, rgh, sc3, *consts)


def _sample_state_body(s_ref, c_ref, b_ref, xw_ref, cd_ref, so_ref, yoff_ref, *, nseq, steps, inner):
    m = steps * nseq
    gw = inner // SSD_GROUPS
    cb = c_ref[...].reshape(m, c_ref.shape[-1]).astype(BF16)
    bb = b_ref[...].reshape(m, b_ref.shape[-1]).astype(BF16)
    xw = xw_ref[...].reshape(m, inner).astype(BF16)
    rowseq = lax.broadcasted_iota(jnp.int32, (m, 1), 0) % nseq
    pad = jnp.zeros((LANES - nseq, gw), F32)
    for g in range(SSD_GROUPS):
        cg = cb[:, g * SSD_STATE:(g + 1) * SSD_STATE]
        bg = bb[:, g * SSD_STATE:(g + 1) * SSD_STATE]
        xg = xw[:, g * gw:(g + 1) * gw]
        cd_t = jnp.concatenate([cd_ref[:, g * gw:(g + 1) * gw], pad], axis=0).T
        yoff = jnp.zeros((m, gw), F32)
        for q in range(nseq):
            mine = rowseq == q
            s0 = s_ref[q, pl.ds(g * gw, gw), :]
            res = lax.dot_general(cg, s0.astype(BF16), (((1,), (1,)), ((), ())), preferred_element_type=F32)
            yoff = yoff + jnp.where(mine, res, 0.0)
            xq = jnp.where(mine, xg, jnp.zeros_like(xg))
            sadd = lax.dot_general(xq, bg, (((0,), (0,)), ((), ())), preferred_element_type=F32)
            so_ref[q, pl.ds(g * gw, gw), :] = s0 * cd_t[:, q:q + 1] + sadd
        yoff_ref[:, :, pl.ds(g * gw, gw)] = yoff.reshape(steps, nseq, gw)


def _sample_state_call(state, c3, b3, xw3, cd, nseq_blk):
    ns, inner, n = state.shape
    steps = c3.shape[0]
    gn = c3.shape[-1]
    assert ns % nseq_blk == 0 and nseq_blk % SUBLANES == 0
    blk3 = lambda w: pl.BlockSpec((steps, nseq_blk, w), lambda i: (0, i, 0))
    sblk = pl.BlockSpec((nseq_blk, inner, n), lambda i: (i, 0, 0))
    body = functools.partial(_sample_state_body, nseq=nseq_blk, steps=steps, inner=inner)
    return pl.pallas_call(
        body,
        grid=(ns // nseq_blk,),
        in_specs=[sblk, blk3(gn), blk3(gn), blk3(inner), pl.BlockSpec((nseq_blk, inner), lambda i: (i, 0))],
        out_specs=(sblk, blk3(inner)),
        out_shape=(jax.ShapeDtypeStruct(state.shape, F32), jax.ShapeDtypeStruct((steps, ns, inner), F32)),
        compiler_params=pltpu.CompilerParams(dimension_semantics=("arbitrary",),
                                             vmem_limit_bytes=VMEM_LIMIT_BYTES),
        name="sample_state",
    )(state, c3, b3, xw3, cd)


def _sample_back_body(x_ref, mrg_ref, yp_ref, yoff_ref, ea_ref, sz_ref, sg_ref,
                      nw_ref, wpssd_ref, wout_ref, npost_ref, o_ref):
    y = (yp_ref[...] + yoff_ref[...] * ea_ref[...]) * sz_ref[...]
    y_ssd = _group_rmsnorm(y, nw_ref[...], SSD_GROUPS)
    m = mrg_ref[...] + sg_ref[...] * _dot(y_ssd.astype(BF16), wpssd_ref[...])
    out = _dot(m.astype(BF16), wout_ref[...])
    o_ref[...] = x_ref[...] + _rms(out, npost_ref[...])


def _sample_back_call(x2, mrg, yp, yoff, ea, sz, sg, p, tm):
    t, d = x2.shape
    assert t % tm == 0
    consts = [p[n] for n in ("nw", "wpssd", "wout", "npost")]
    rows = lambda a: pl.BlockSpec((tm, a.shape[1]), lambda i: (i, 0))
    acts = [x2, mrg, yp, yoff, ea, sz, sg]
    return pl.pallas_call(
        _sample_back_body,
        grid=(t // tm,),
        in_specs=[rows(a) for a in acts] + [_const_spec(c.shape) for c in consts],
        out_specs=pl.BlockSpec((tm, d), lambda i: (i, 0)),
        out_shape=jax.ShapeDtypeStruct((t, d), F32),
        compiler_params=pltpu.CompilerParams(dimension_semantics=("arbitrary",),
                                             vmem_limit_bytes=VMEM_LIMIT_BYTES),
        name="sample_back",
    )(*acts, *consts)


def _prep_mixer_params(n_mix_pre, n_mix_post, w_in, rg_conv_w, rg_conv_b, rg_wa, rg_ba, rg_wx, rg_bx,
                       rg_lambda, ssd_conv_w, ssd_conv_b, ssd_dt_bias, ssd_a_log, ssd_d, ssd_norm_w,
                       w_proj_rg, w_proj_ssd, w_out):
    width = rg_conv_w.shape[1]
    heads = ssd_dt_bias.shape[0]
    inner = ssd_norm_w.shape[0]
    conv_dim = ssd_conv_w.shape[1]
    main = 2 * width + inner + conv_dim
    row = lambda v: v.reshape(1, -1).astype(F32)
    padl = lambda v: jnp.pad(v.reshape(1, -1).astype(F32), ((0, 0), (0, LANES - v.shape[-1])))
    e = (jnp.arange(LANES)[:, None] == (jnp.arange(inner)[None, :] // SSD_HEAD_DIM)).astype(BF16)
    head_group = jnp.where(jnp.arange(LANES) < heads, jnp.arange(LANES) // (heads // SSD_GROUPS), -1)
    gsum = ((jnp.arange(SSD_GROUPS * SSD_STATE)[:, None] // SSD_STATE) == head_group[None, :]).astype(BF16)
    d_in = w_in.shape[0]
    w_bf = w_in.astype(BF16)
    assert main % LANES == 0 and main + LANES <= w_in.shape[1]
    views = dict(wmain=((d_in, main), 0), wdt=((d_in, LANES), main // LANES))
    return dict(
        gsum=gsum, views=views,
        npre=row(n_mix_pre), npost=row(n_mix_post),
        wmain=w_bf, wdt=w_bf, wgate=w_bf[:, main + heads:],
        rcw=rg_conv_w.astype(F32), rcb=row(rg_conv_b),
        wcat=jnp.concatenate([rg_wa, rg_wx], axis=-1).astype(BF16),
        ba=row(rg_ba), bx=row(rg_bx), lam=row(rg_lambda),
        scw=ssd_conv_w.astype(F32), scb=row(ssd_conv_b),
        dtb=padl(ssd_dt_bias), alog=padl(ssd_a_log),
        dexp=row(jnp.repeat(ssd_d, SSD_HEAD_DIM)), nw=row(ssd_norm_w), e=e,
        wprg=w_proj_rg.astype(BF16), wpssd=w_proj_ssd.astype(BF16), wout=w_out.astype(BF16),
    )


FFN_TILE = 512
PROMPT_TILE = 256
SAMPLE_FRONT_SEQS = 32
SAMPLE_STATE_SEQS = 8
SAMPLE_BACK_TILE = 256


def _prompt_layer(x, mp, f1, f2):
    nb, seq, d = x.shape
    x = _ffn_call(x.reshape(nb * seq, d), *f1, min(FFN_TILE, nb * seq)).reshape(nb, seq, d)
    x, rgh, rgc, ssd, ssdc = _prompt_mixer_call(x, mp, min(PROMPT_TILE, seq))
    x = _ffn_call(x.reshape(nb * seq, d), *f2, min(FFN_TILE, nb * seq)).reshape(nb, seq, d)
    heads = ssd.shape[2] // SSD_HEAD_DIM
    return x, (rgh[0], rgc[0], ssd[0].reshape(nb, heads, SSD_HEAD_DIM, SSD_STATE), ssdc[0])


def _sample_layer(x, rg_h, rg_conv, ssd, ssd_conv, mp, f1, f2):
    ns, steps, d = x.shape
    heads, hd, n = ssd.shape[1:]
    t = steps * ns
    to_steps = lambda v: jnp.transpose(v, (1, 0, 2))
    flat = lambda v: v.reshape(t, v.shape[-1])
    x = _ffn_call(to_steps(x).reshape(t, d), *f1, min(FFN_TILE, t))
    (mrg, rgh_new, rgc_new, sc_new, c3, b3, xw3, cd, yp3, ea3, sz3, sg3) = _sample_front_call(
        x.reshape(steps, ns, d), to_steps(rg_conv), rg_h, to_steps(ssd_conv), mp, min(SAMPLE_FRONT_SEQS, ns))
    ssd_new, yoff3 = _sample_state_call(ssd.reshape(ns, heads * hd, n), c3, b3, xw3, cd, SAMPLE_STATE_SEQS)
    x = _sample_back_call(x, flat(mrg), flat(yp3), flat(yoff3), flat(ea3), flat(sz3), flat(sg3), mp,
                          min(SAMPLE_BACK_TILE, t))
    x = _ffn_call(x, *f2, min(FFN_TILE, t))
    return (to_steps(x.reshape(steps, ns, d)),
            (rgh_new, to_steps(rgc_new), ssd_new.reshape(ns, heads, hd, n), to_steps(sc_new)))


def kernel(x_prompt, x_sample, state_rg_h, state_rg_conv, state_ssd, state_ssd_conv, n_ffn1_pre, n_ffn1_post, ffn1_wg, ffn1_wu, ffn1_wd, n_mix_pre, n_mix_post, w_in, rg_conv_w, rg_conv_b, rg_wa, rg_ba, rg_wx, rg_bx, rg_lambda, ssd_conv_w, ssd_conv_b, ssd_dt_bias, ssd_a_log, ssd_d, ssd_norm_w, w_proj_rg, w_proj_ssd, w_out, n_ffn2_pre, n_ffn2_post, ffn2_wg, ffn2_wu, ffn2_wd):
    depth = w_in.shape[0]
    row = lambda v: v.reshape(1, -1).astype(F32)
    yp, ys = x_prompt, x_sample
    p_new = ([], [], [], [])
    s_new = ([], [], [], [])
    for li in range(depth):
        mp = _prep_mixer_params(n_mix_pre[li], n_mix_post[li], w_in[li], rg_conv_w[li], rg_conv_b[li], rg_wa[li],
                                rg_ba[li], rg_wx[li], rg_bx[li], rg_lambda[li], ssd_conv_w[li], ssd_conv_b[li],
                                ssd_dt_bias[li], ssd_a_log[li], ssd_d[li], ssd_norm_w[li], w_proj_rg[li],
                                w_proj_ssd[li], w_out[li])
        f1 = (row(n_ffn1_pre[li]), row(n_ffn1_post[li]), ffn1_wg[li].astype(BF16), ffn1_wu[li].astype(BF16),
              ffn1_wd[li].astype(BF16))
        f2 = (row(n_ffn2_pre[li]), row(n_ffn2_post[li]), ffn2_wg[li].astype(BF16), ffn2_wu[li].astype(BF16),
              ffn2_wd[li].astype(BF16))
        yp, newp = _prompt_layer(yp, mp, f1, f2)
        ys, news = _sample_layer(ys, state_rg_h[li], state_rg_conv[li], state_ssd[li], state_ssd_conv[li], mp, f1, f2)
        for lst, v in zip(p_new, newp):
            lst.append(v)
        for lst, v in zip(s_new, news):
            lst.append(v)
    prompt_state = [jnp.stack(v, 0) for v in p_new]
    sample_state = [jnp.stack(v, 0) for v in s_new]
    return (yp, ys, *prompt_state, *sample_state)
```
